```python
import math
import jax
import jax.numpy as jnp
from jax import lax
import numpy as np


D_MODEL = 2048
BATCH = 8
SEQ = 4096
DEPTH = 4

GRID_W = 64
CTX_LEN = 256
N_MIXERS = 4
CTX_READER_KINDS = (1, 2)
RMS_EPS = 1e-6
F32 = jnp.float32

FN_GROUPS = 4
FN_GROUP_DIM = D_MODEL // FN_GROUPS

HEAD_DIM = 128
N_HEADS = D_MODEL // HEAD_DIM
N_KV_HEADS = 4
GQA_GROUP = N_HEADS // N_KV_HEADS
Q_BLOCK = 128
ROPE_THETA = 10000.0
ROPE_FREQS = HEAD_DIM // 4
ATTN_SCALE = HEAD_DIM ** -0.5
Q_DIM = N_HEADS * HEAD_DIM
KV_DIM = N_KV_HEADS * HEAD_DIM
ATTN_IN_DIM = Q_DIM + 2 * KV_DIM

SSM_INNER = 2 * D_MODEL
SSM_HEAD_DIM = 64
SSM_HEADS = SSM_INNER // SSM_HEAD_DIM
SSM_GROUPS = 8
SSM_HEADS_PER_GROUP = SSM_HEADS // SSM_GROUPS
SSM_STATE = 128
SSM_CONV = 3
SSM_CHUNK = 128
SSM_BC_DIM = SSM_GROUPS * SSM_STATE
SSM_CONV_CH = SSM_INNER + 2 * SSM_BC_DIM
SSM_IN_DIM = SSM_INNER + SSM_CONV_CH + 2 * SSM_HEADS
SSM_DT_MIN = 1e-3
SSM_DT_MAX = 1e-1

HY_ORDER = 2
HY_EMB = 33
HY_BANDS = (HY_EMB - 1) // 2
HY_FILTER_HIDDEN = 64
HY_SHORT_CONV = 3
HY_DECAY_TARGET = 1e-2
HY_DECAY_FAST = 0.3
HY_DECAY_SLOW = 1.5

D_FF = 5632
FFN_CONV = 3

kernel_name = 'hybrid_interleaved_dit_block'


def rms_norm(t, w):
    t32 = t.astype(F32)
    y = t32 * lax.rsqrt(jnp.mean(t32 * t32, axis=-1, keepdims=True) + RMS_EPS)
    return (y * w.astype(F32)).astype(t.dtype)


def dwconv(t, w, b):
    width, ch = w.shape
    pad = width // 2
    y = lax.conv_general_dilated(t, w.astype(t.dtype)[:, None, :], window_strides=(1,), padding=[(pad, pad)],
                                 dimension_numbers=('NWC', 'WIO', 'NWC'), feature_group_count=ch)
    return y + b.astype(t.dtype)


def modulation(cvec, w, b):
    m = jax.nn.silu(cvec) @ w + b
    return jnp.split(m, 6, axis=-1)


def axial_rope_tables(n_tokens):
    rows = n_tokens // GRID_W
    row = jnp.broadcast_to(jnp.arange(rows, dtype=F32)[:, None], (rows, GRID_W)).reshape(n_tokens)
    col = jnp.broadcast_to(jnp.arange(GRID_W, dtype=F32)[None, :], (rows, GRID_W)).reshape(n_tokens)
    inv_freq = ROPE_THETA ** (-jnp.arange(ROPE_FREQS, dtype=F32) / ROPE_FREQS)
    ang = jnp.stack([row[:, None] * inv_freq, col[:, None] * inv_freq], axis=1)
    return jnp.cos(ang), jnp.sin(ang)


def apply_axial_rope(t, cos, sin):
    b, n, h, d = t.shape
    t32 = t.astype(F32).reshape(b, n, h, 2, 2, ROPE_FREQS)
    t1, t2 = t32[..., 0, :], t32[..., 1, :]
    c = cos[None, :, None]
    s = sin[None, :, None]
    out = jnp.stack([t1 * c - t2 * s, t2 * c + t1 * s], axis=-2)
    return out.reshape(b, n, h, d).astype(t.dtype)


def gqa_attend(qg, k, v):
    s = jnp.einsum('bqkgd,bskd->bkgqs', qg, k, preferred_element_type=F32) * ATTN_SCALE
    p = jax.nn.softmax(s, axis=-1).astype(v.dtype)
    return jnp.einsum('bkgqs,bskd->bqkgd', p, v)


def blocked_attention(q, k, v):
    b, n = q.shape[:2]
    nb = n // Q_BLOCK
    qb = q.reshape(b, nb, Q_BLOCK, N_KV_HEADS, GQA_GROUP, HEAD_DIM).transpose(1, 0, 2, 3, 4, 5)
    ob = lax.map(lambda blk: gqa_attend(blk, k, v), qb)
    return ob.transpose(1, 0, 2, 3, 4, 5).reshape(b, n, Q_DIM)


def fourier_mixer(u, w_in, w_out):
    b, n, _ = u.shape
    z = (u @ w_in).reshape(b, n, FN_GROUPS, FN_GROUP_DIM)
    f = jnp.fft.fft2(z.astype(F32), axes=(1, 3), norm='ortho').real
    return f.astype(u.dtype).reshape(b, n, D_MODEL) @ w_out


def attention_mixer(u, uc, need_ctx_out, w_in, q_gain, k_gain, w_out, rope_cos, rope_sin):
    def project(t):
        b, n, _ = t.shape
        p = t @ w_in
        q = rms_norm(p[..., :Q_DIM].reshape(b, n, N_HEADS, HEAD_DIM), q_gain)
        k = rms_norm(p[..., Q_DIM:Q_DIM + KV_DIM].reshape(b, n, N_KV_HEADS, HEAD_DIM), k_gain)
        v = p[..., Q_DIM + KV_DIM:].reshape(b, n, N_KV_HEADS, HEAD_DIM)
        return q, k, v

    q, k, v = project(u)
    q = apply_axial_rope(q, rope_cos, rope_sin)
    k = apply_axial_rope(k, rope_cos, rope_sin)
    qc, kc, vc = project(uc)
    k_all = jnp.concatenate([kc, k], axis=1)
    v_all = jnp.concatenate([vc, v], axis=1)
    o = blocked_attention(q, k_all, v_all) @ w_out
    oc = None
    if need_ctx_out:
        b, nc = qc.shape[:2]
        qcg = qc.reshape(b, nc, N_KV_HEADS, GQA_GROUP, HEAD_DIM)
        oc = gqa_attend(qcg, kc, vc).reshape(b, nc, Q_DIM) @ w_out
    return o, oc


def ssd_project(u, w_in, cw, cb):
    b, n, _ = u.shape
    p = u @ w_in
    z = p[..., :SSM_INNER]
    xbc = jax.nn.silu(dwconv(p[..., SSM_INNER:SSM_INNER + SSM_CONV_CH], cw, cb))
    dt = p[..., SSM_INNER + SSM_CONV_CH:]
    xs = xbc[..., :SSM_INNER].reshape(b, n, SSM_HEADS, SSM_HEAD_DIM)
    bm = xbc[..., SSM_INNER:SSM_INNER + SSM_BC_DIM].reshape(b, n, SSM_GROUPS, SSM_STATE)
    cm = xbc[..., SSM_INNER + SSM_BC_DIM:].reshape(b, n, SSM_GROUPS, SSM_STATE)
    return z, xs, bm, cm, dt


def ssd_chunk_scan(x, dt, a, bm, cm, s0, with_output):
    bsz, n = x.shape[:2]
    nc = n // SSM_CHUNK
    g, hg = SSM_GROUPS, SSM_HEADS_PER_GROUP

    def chunks(t, tail):
        return jnp.moveaxis(t.astype(F32).reshape((bsz, nc, SSM_CHUNK) + tail), 1, 0)

    xc = chunks(x, (g, hg, SSM_HEAD_DIM))
    dtc = chunks(dt, (g, hg))
    ac = dtc * a.reshape(g, hg)
    bc = chunks(bm, (g, SSM_STATE))
    cc = chunks(cm, (g, SSM_STATE))
    lower = jnp.tril(jnp.ones((SSM_CHUNK, SSM_CHUNK), bool))[None, :, :, None, None]

    def step(s, inp):
        xq, dq, aq, bq, cq = inp
        acs = jnp.cumsum(aq, axis=1)
        last = acs[:, -1]
        wend = jnp.exp(last[:, None] - acs) * dq
        s_new = s * jnp.exp(last)[..., None, None] + jnp.einsum('bjghp,bjgn->bghpn', wend[..., None] * xq, bq)
        if not with_output:
            return s_new, None
        seg = acs[:, :, None] - acs[:, None, :]
        decay = jnp.exp(jnp.where(lower, seg, -jnp.inf))
        cb = jnp.einsum('bign,bjgn->bijg', cq, bq)
        w = cb[..., None] * decay * dq[:, None]
        y = jnp.einsum('bijgh,bjghp->bighp', w, xq)
        y = y + jnp.einsum('bign,bghpn->bighp', cq, s) * jnp.exp(acs)[..., None]
        return s_new, y

    s_fin, ys = lax.scan(step, s0, (xc, dtc, ac, bc, cc))
    if not with_output:
        return None, s_fin
    y = jnp.moveaxis(ys, 0, 1).reshape(bsz, n, SSM_HEADS, SSM_HEAD_DIM)
    return y, s_fin


def bidir_ssd(xs, bm, cm, dt, dt_bias, a_log, s0_fwd, s0_bwd, with_output):
    a = -jnp.exp(a_log.astype(F32))
    dt32 = dt.astype(F32)
    dt_f = jax.nn.softplus(dt32[..., :SSM_HEADS] + dt_bias[0].astype(F32))
    dt_b = jax.nn.softplus(dt32[..., SSM_HEADS:] + dt_bias[1].astype(F32))
    flip = lambda t: jnp.flip(t, axis=1)
    y_f, s_f = ssd_chunk_scan(xs, dt_f, a[0], bm, cm, s0_fwd, with_output)
    y_b, s_b = ssd_chunk_scan(flip(xs), flip(dt_b), a[1], flip(bm), flip(cm), s0_bwd, with_output)
    y = y_f + flip(y_b) if with_output else None
    return y, s_f, s_b


def ssd_output(y, xs, z, d_skip, norm_w, w_out, dtype):
    b, n = y.shape[:2]
    y = (y + xs.astype(F32) * d_skip.astype(F32)[:, None]).reshape(b, n, SSM_INNER)
    y = y * jax.nn.silu(z.astype(F32))
    yg = y.reshape(b, n, SSM_GROUPS, SSM_INNER // SSM_GROUPS)
    yg = yg * lax.rsqrt(jnp.mean(yg * yg, axis=-1, keepdims=True) + RMS_EPS)
    y = yg.reshape(b, n, SSM_INNER) * norm_w.astype(F32)
    return y.astype(dtype) @ w_out


def ssd_mixer(u, uc, need_ctx_out, w_in, cw, cb, dt_bias, a_log, d_skip, norm_w, w_out):
    z, xs, bm, cm, dt = ssd_project(u, w_in, cw, cb)
    zc, xsc, bmc, cmc, dtc = ssd_project(uc, w_in, cw, cb)
    s0 = jnp.zeros((u.shape[0], SSM_GROUPS, SSM_HEADS_PER_GROUP, SSM_HEAD_DIM, SSM_STATE), F32)
    yc, s_f, s_b = bidir_ssd(xsc, bmc, cmc, dtc, dt_bias, a_log, s0, s0, need_ctx_out)
    y, _, _ = bidir_ssd(xs, bm, cm, dt, dt_bias, a_log, s_f, s_b, True)
    o = ssd_output(y, xs, z, d_skip, norm_w, w_out, u.dtype)
    oc = ssd_output(yc, xsc, zc, d_skip, norm_w, w_out, uc.dtype) if need_ctx_out else None
    return o, oc


def hyena_filter_spectrum(n, w1, b1, w2, b2, w3, freq, decay):
    t01 = jnp.linspace(0.0, 1.0, n, dtype=F32)[:, None]
    w = (2.0 * math.pi / n) * jnp.arange(n, dtype=F32)[:, None]
    f = jnp.linspace(1e-4, HY_BANDS - 1, HY_BANDS, dtype=F32)[None, :]
    feats = jnp.concatenate([t01, jnp.cos(f * w), -jnp.sin(f * w)], axis=-1)
    fr = freq.astype(F32)
    hid = jnp.sin(fr * (feats @ w1.astype(F32) + b1.astype(F32)))
    hid = jnp.sin(fr * (hid @ w2.astype(F32) + b2.astype(F32)))
    taps = (hid @ w3.astype(F32)) * jnp.exp(-t01 * jnp.abs(decay.astype(F32)))
    taps = taps.reshape(n, HY_ORDER, 2, D_MODEL)
    two_sided = jnp.concatenate([taps[:, :, 0], jnp.zeros((1, HY_ORDER, D_MODEL), F32), taps[:0:-1, :, 1]], axis=0)
    return jnp.fft.rfft(two_sided, axis=0)


def long_conv(t, spec, bias):
    n = t.shape[1]
    t32 = t.astype(F32)
    y = jnp.fft.irfft(jnp.fft.rfft(t32, n=2 * n, axis=1) * spec[None], n=2 * n, axis=1)[:, :n]
    return (y + t32 * bias.astype(F32)).astype(t.dtype)


def hyena_mixer(u, w_in, cw, cb, fw1, fb1, fw2, fb2, fw3, ffreq, decay, fbias, w_out):
    p = dwconv(u @ w_in, cw, cb)
    v, x1, x2 = jnp.split(p, 3, axis=-1)
    spec = hyena_filter_spectrum(u.shape[1], fw1, fb1, fw2, fb2, fw3, ffreq, decay)
    z = x1 * long_conv(v, spec[:, 0], fbias[0])
    y = x2 * long_conv(z, spec[:, 1], fbias[1])
    return y @ w_out


def conv_ffn(t, w_up, cw, cb, w_down):
    p = dwconv(t @ w_up, cw, cb)
    g, val = jnp.split(p, 2, axis=-1)
    return (jax.nn.silu(g) * val) @ w_down


def setup_inputs(seed: int = 0) -> dict:
    key = jax.random.key(seed)
    keys = iter(jax.random.split(key, 64))

    def normal(shape, scale):
        return scale * jax.random.normal(next(keys), shape, F32)

    def gain(shape):
        return 1.0 + normal(shape, 0.02)

    D = D_MODEL
    n_a, n_b, n_c, n_d = (len(range(k, DEPTH, N_MIXERS)) for k in range(N_MIXERS))
    dt0 = jnp.exp(jax.random.uniform(next(keys), (n_c, 2, SSM_HEADS), F32)
                  * (math.log(SSM_DT_MAX) - math.log(SSM_DT_MIN)) + math.log(SSM_DT_MIN))
    decay_lo = -math.log(HY_DECAY_TARGET) / HY_DECAY_SLOW
    decay_hi = -math.log(HY_DECAY_TARGET) / HY_DECAY_FAST
    return {
        'x': normal((BATCH, SEQ, D), 1.0),
        'c': normal((BATCH, D), 1.0),
        'ctx': normal((BATCH, CTX_LEN, D), 1.0),
        'c_ctx': normal((D,), 1.0),
        'ada_w': normal((DEPTH, D, 6 * D), 0.5 * D ** -0.5),
        'ada_b': normal((DEPTH, 6 * D), 0.02),
        'norm_pre_mix': gain((DEPTH, D)),
        'norm_post_mix': gain((DEPTH, D)),
        'norm_pre_ffn': gain((DEPTH, D)),
        'norm_post_ffn': gain((DEPTH, D)),
        'ffn_up': normal((DEPTH, D, 2 * D_FF), D ** -0.5),
        'ffn_conv_w': normal((DEPTH, FFN_CONV, 2 * D_FF), FFN_CONV ** -0.5),
        'ffn_conv_b': normal((DEPTH, 2 * D_FF), 0.02),
        'ffn_down': normal((DEPTH, D_FF, D), D_FF ** -0.5),
        'fn_in': normal((n_a, D, D), D ** -0.5),
        'fn_out': normal((n_a, D, D), D ** -0.5),
        'attn_in': normal((n_b, D, ATTN_IN_DIM), D ** -0.5),
        'attn_q_gain': gain((n_b, HEAD_DIM)),
        'attn_k_gain': gain((n_b, HEAD_DIM)),
        'attn_out': normal((n_b, Q_DIM, D), Q_DIM ** -0.5),
        'ssm_in': normal((n_c, D, SSM_IN_DIM), D ** -0.5),
        'ssm_conv_w': normal((n_c, SSM_CONV, SSM_CONV_CH), SSM_CONV ** -0.5),
        'ssm_conv_b': normal((n_c, SSM_CONV_CH), 0.02),
        'ssm_dt_bias': dt0 + jnp.log(-jnp.expm1(-dt0)),
        'ssm_a_log': jnp.log(jax.random.uniform(next(keys), (n_c, 2, SSM_HEADS), F32, 1.0, 16.0)),
        'ssm_d': gain((n_c, SSM_HEADS)),
        'ssm_norm': gain((n_c, SSM_INNER)),
        'ssm_out': normal((n_c, SSM_INNER, D), SSM_INNER ** -0.5),
        'hy_in': normal((n_d, D, 3 * D), D ** -0.5),
        'hy_conv_w': normal((n_d, HY_SHORT_CONV, 3 * D), HY_SHORT_CONV ** -0.5),
        'hy_conv_b': normal((n_d, 3 * D), 0.02),
        'hy_f_w1': normal((n_d, HY_EMB, HY_FILTER_HIDDEN), HY_EMB ** -0.5),
        'hy_f_b1': normal((n_d, HY_FILTER_HIDDEN), 0.02),
        'hy_f_w2': normal((n_d, HY_FILTER_HIDDEN, HY_FILTER_HIDDEN), HY_FILTER_HIDDEN ** -0.5),
        'hy_f_b2': normal((n_d, HY_FILTER_HIDDEN), 0.02),
        'hy_f_w3': normal((n_d, HY_FILTER_HIDDEN, HY_ORDER * 2 * D), HY_FILTER_HIDDEN ** -0.5),
        'hy_f_freq': gain((n_d, HY_FILTER_HIDDEN)),
        'hy_decay': jax.random.uniform(next(keys), (n_d, HY_ORDER * 2 * D), F32, decay_lo, decay_hi),
        'hy_f_bias': normal((n_d, HY_ORDER, D), 1.0),
        'hy_out': normal((n_d, D, D), D ** -0.5),
    }


def reference(x, c, ctx, c_ctx, ada_w, ada_b, norm_pre_mix, norm_post_mix, norm_pre_ffn, norm_post_ffn,
              ffn_up, ffn_conv_w, ffn_conv_b, ffn_down, fn_in, fn_out, attn_in, attn_q_gain, attn_k_gain,
              attn_out, ssm_in, ssm_conv_w, ssm_conv_b, ssm_dt_bias, ssm_a_log, ssm_d, ssm_norm, ssm_out,
              hy_in, hy_conv_w, hy_conv_b, hy_f_w1, hy_f_b1, hy_f_w2, hy_f_b2, hy_f_w3, hy_f_freq,
              hy_decay, hy_f_bias, hy_out):
    n_tok = x.shape[1]
    rope_cos, rope_sin = axial_rope_tables(n_tok)
    readers = [i for i in range(DEPTH) if i % N_MIXERS in CTX_READER_KINDS]
    last_reader = max(readers) if readers else -1
    h, hc = x, ctx
    for i in range(DEPTH):
        kind, j = i % N_MIXERS, i // N_MIXERS
        ctx_live = i <= last_reader
        ctx_next = i < last_reader
        sh1, sc1, g1, sh2, sc2, g2 = [m[:, None, :] for m in modulation(c, ada_w[i], ada_b[i])]
        u = rms_norm(h, norm_pre_mix[i]) * (1 + sc1) + sh1
        uc = None
        if ctx_live:
            csh1, csc1, cg1, csh2, csc2, cg2 = modulation(c_ctx, ada_w[i], ada_b[i])
            uc = rms_norm(hc, norm_pre_mix[i]) * (1 + csc1) + csh1
        oc = None
        if kind == 0:
            o = fourier_mixer(u, fn_in[j], fn_out[j])
            if ctx_next:
                oc = fourier_mixer(uc, fn_in[j], fn_out[j])
        elif kind == 1:
            o, oc = attention_mixer(u, uc, ctx_next, attn_in[j], attn_q_gain[j], attn_k_gain[j], attn_out[j],
                                    rope_cos, rope_sin)
        elif kind == 2:
            o, oc = ssd_mixer(u, uc, ctx_next, ssm_in[j], ssm_conv_w[j], ssm_conv_b[j], ssm_dt_bias[j],
                              ssm_a_log[j], ssm_d[j], ssm_norm[j], ssm_out[j])
        else:
            hy_args = (hy_in[j], hy_conv_w[j], hy_conv_b[j], hy_f_w1[j], hy_f_b1[j], hy_f_w2[j], hy_f_b2[j],
                       hy_f_w3[j], hy_f_freq[j], hy_decay[j], hy_f_bias[j], hy_out[j])
            o = hyena_mixer(u, *hy_args)
            if ctx_next:
                oc = hyena_mixer(uc, *hy_args)
        h = h + g1 * rms_norm(o, norm_post_mix[i])
        f = conv_ffn(rms_norm(h, norm_pre_ffn[i]) * (1 + sc2) + sh2, ffn_up[i], ffn_conv_w[i], ffn_conv_b[i], ffn_down[i])
        h = h + g2 * rms_norm(f, norm_post_ffn[i])
        if ctx_next:
            hc = hc + cg1 * rms_norm(oc, norm_post_mix[i])
            fc = conv_ffn(rms_norm(hc, norm_pre_ffn[i]) * (1 + csc2) + csh2, ffn_up[i], ffn_conv_w[i], ffn_conv_b[i], ffn_down[i])
            hc = hc + cg2 * rms_norm(fc, norm_post_ffn[i])
    return h
```

```python
import functools
import math

import jax
import jax.numpy as jnp
from jax import lax
from jax.experimental import pallas as pl
from jax.experimental.pallas import tpu as pltpu

F32 = jnp.float32
BF16 = jnp.bfloat16

GRID_W = 64
N_MIXERS = 4
CTX_READER_KINDS = (1, 2)
RMS_EPS = 1e-6
FN_GROUPS = 4
HEAD_DIM = 128
N_KV_HEADS = 4
ROPE_THETA = 10000.0
ROPE_FREQS = HEAD_DIM // 4
ATTN_SCALE = HEAD_DIM ** -0.5
SSM_HEAD_DIM = 64
SSM_GROUPS = 8
SSM_STATE = 128
SSM_CHUNK = 128
HY_ORDER = 2
HY_EMB = 33
HY_BANDS = (HY_EMB - 1) // 2

V7X_VMEM_LIMIT_BYTES = 56 * 1024 * 1024
LANES = 128
CONV_HALO = 16
NEG_BIG = -1e30


def _cparams(*sem):
    return pltpu.CompilerParams(dimension_semantics=sem, vmem_limit_bytes=V7X_VMEM_LIMIT_BYTES)


def _tile(n, pref, mult=LANES):
    if n <= pref:
        return n
    t = (pref // mult) * mult
    while t >= mult:
        if n % t == 0:
            return t
        t -= mult
    return n


def _silu(x):
    return x * (1.0 / (1.0 + jnp.exp(-x)))


def _split_bf16(v):
    hi = v.astype(BF16)
    lo = (v - hi.astype(F32)).astype(BF16)
    return hi, lo


def _dot(a, b):
    return jnp.dot(a, b, preferred_element_type=F32)


def _dot_nt(a, b):
    return lax.dot_general(a, b, (((1,), (1,)), ((), ())), preferred_element_type=F32)


def _dot3(a, b):
    ah, al = _split_bf16(a)
    bh, bl = _split_bf16(b)
    return _dot(ah, bh) + _dot(ah, bl) + _dot(al, bh)


def _rms_rows(x, w):
    return x * lax.rsqrt(jnp.mean(x * x, axis=-1, keepdims=True) + RMS_EPS) * w


def _mod_kernel(c_ref, w_ref, b_ref, o_ref):
    a = _silu(c_ref[...]).astype(BF16)
    o_ref[0] = _dot(a, w_ref[0].astype(BF16)) + b_ref[0]


def _modulation_all(c_rows, ada_w, ada_b):
    depth, d, n6 = ada_w.shape
    r = c_rows.shape[0]
    tn = _tile(n6, 1024)
    return pl.pallas_call(
        _mod_kernel,
        grid=(depth, n6 // tn),
        in_specs=[
            pl.BlockSpec((r, d), lambda l, j: (0, 0)),
            pl.BlockSpec((1, d, tn), lambda l, j: (l, 0, j)),
            pl.BlockSpec((1, 1, tn), lambda l, j: (l, 0, j)),
        ],
        out_specs=pl.BlockSpec((1, r, tn), lambda l, j: (l, 0, j)),
        out_shape=jax.ShapeDtypeStruct((depth, r, n6), F32),
        compiler_params=_cparams("parallel", "parallel"),
        name="ada_mod",
    )(c_rows, ada_w, ada_b.reshape(depth, 1, n6))


def _proj_kernel(*refs, n_w, use_conv, n_aux, n_out, epi, tm, n_row_tiles):
    refs = list(refs)
    h_ref = refs.pop(0)
    if use_conv:
        hp_ref = refs.pop(0)
        hn_ref = refs.pop(0)
    nw_ref, sc_ref, sh_ref = refs.pop(0), refs.pop(0), refs.pop(0)
    w_refs = [refs.pop(0) for _ in range(n_w)]
    cw_refs, cb_refs = [], []
    if use_conv:
        for _ in range(n_w):
            cw_refs.append(refs.pop(0))
            cb_refs.append(refs.pop(0))
    aux_refs = [refs.pop(0) for _ in range(n_aux)]
    out_refs = [refs.pop(0) for _ in range(n_out)]
    u_ref = refs.pop(0)
    p_ref = refs.pop(0) if use_conv else None
    i = pl.program_id(1)
    j = pl.program_id(2)
    hl = CONV_HALO

    @pl.when(j == 0)
    def _():
        nwv = nw_ref[...]
        scv = 1.0 + sc_ref[0]
        shv = sh_ref[0]

        def nm(x):
            return _rms_rows(x, nwv) * scv + shv

        if use_conv:
            u_ref[hl:hl + tm, :] = nm(h_ref[0]).astype(BF16)
            u_ref[0:hl, :] = jnp.where(i == 0, 0.0, nm(hp_ref[0])).astype(BF16)
            u_ref[hl + tm:, :] = jnp.where(i == n_row_tiles - 1, 0.0, nm(hn_ref[0])).astype(BF16)
        else:
            u_ref[...] = nm(h_ref[0]).astype(BF16)

    ys = []
    for k in range(n_w):
        p = _dot(u_ref[...], w_refs[k][...])
        if use_conv:
            p_ref[...] = p
            cw = cw_refs[k][...]
            y = (cw[0:1] * p_ref[hl - 1:hl - 1 + tm, :] + cw[1:2] * p_ref[hl:hl + tm, :]
                 + cw[2:3] * p_ref[hl + 1:hl + 1 + tm, :] + cb_refs[k][...])
        else:
            y = p
        ys.append(y)
    epi(j, ys, aux_refs, out_refs)


def _proj(h, nw, sc, sh, w_views, *, n_col_tiles, tn, tm, epi, out_shapes, out_specs,
          conv=None, aux=(), aux_specs=(), name):
    b, t, d = h.shape
    tm = _tile(t, tm, 16)
    n_row_tiles = t // tm
    use_conv = conv is not None
    hl = CONV_HALO
    ins = [h]
    specs = [pl.BlockSpec((1, tm, d), lambda bi, i, j: (bi, i, 0))]
    if use_conv:
        r = tm // hl
        last = t // hl - 1
        ins += [h, h]
        specs += [
            pl.BlockSpec((1, hl, d), lambda bi, i, j: (bi, jnp.maximum(i * r - 1, 0), 0)),
            pl.BlockSpec((1, hl, d), lambda bi, i, j: (bi, jnp.minimum((i + 1) * r, last), 0)),
        ]
    ins += [nw.reshape(1, d), sc, sh]
    specs += [
        pl.BlockSpec((1, d), lambda bi, i, j: (0, 0)),
        pl.BlockSpec((1, 1, d), lambda bi, i, j: (bi, 0, 0)),
        pl.BlockSpec((1, 1, d), lambda bi, i, j: (bi, 0, 0)),
    ]
    for w, off in w_views:
        ins.append(w)
        specs.append(pl.BlockSpec((d, tn), lambda bi, i, j, off=off: (0, off + j)))
    if use_conv:
        cw, cb = conv
        for _, off in w_views:
            ins += [cw, cb]
            specs += [
                pl.BlockSpec((cw.shape[0], tn), lambda bi, i, j, off=off: (0, off + j)),
                pl.BlockSpec((1, tn), lambda bi, i, j, off=off: (0, off + j)),
            ]
    ins += list(aux)
    specs += list(aux_specs)
    rows = tm + 2 * hl if use_conv else tm
    scratch = [pltpu.VMEM((rows, d), BF16)]
    if use_conv:
        scratch.append(pltpu.VMEM((rows, tn), F32))
    kern = functools.partial(_proj_kernel, n_w=len(w_views), use_conv=use_conv, n_aux=len(aux),
                             n_out=len(out_shapes), epi=epi, tm=tm, n_row_tiles=n_row_tiles)
    return pl.pallas_call(
        kern,
        grid=(b, n_row_tiles, n_col_tiles),
        in_specs=specs,
        out_specs=out_specs(tm),
        out_shape=out_shapes,
        scratch_shapes=scratch,
        compiler_params=_cparams("parallel", "parallel", "arbitrary"),
        name=name,
    )(*ins)


def _std_out(b, t, n, tn, dtype):
    shapes = [jax.ShapeDtypeStruct((b, t, n), dtype)]
    specs = lambda tm: [pl.BlockSpec((1, tm, tn), lambda bi, i, j: (bi, i, j))]
    return shapes, specs


def _epi_identity(j, ys, aux_refs, out_refs):
    out_refs[0][0] = ys[0].astype(out_refs[0].dtype)


def _epi_silu(j, ys, aux_refs, out_refs):
    out_refs[0][0] = _silu(ys[0]).astype(out_refs[0].dtype)


def _epi_swiglu(j, ys, aux_refs, out_refs):
    out_refs[0][0] = (_silu(ys[0]) * ys[1]).astype(out_refs[0].dtype)


def _outproj_kernel(a_ref, w_ref, h_ref, g_ref, nw_ref, o_ref, acc_ref, *, nk):
    k = pl.program_id(2)
    part = _dot(a_ref[0], w_ref[...])

    @pl.when(k == 0)
    def _():
        acc_ref[...] = part

    @pl.when(k > 0)
    def _():
        acc_ref[...] += part

    @pl.when(k == nk - 1)
    def _():
        o_ref[0] = h_ref[0] + g_ref[0] * _rms_rows(acc_ref[...], nw_ref[...])


def _outproj(a, w, h, g, nw, *, tm=512, tk=512, name):
    b, t, kdim = a.shape
    d = w.shape[1]
    tm = _tile(t, tm, 16)
    tk = _tile(kdim, tk)
    nk = kdim // tk
    return pl.pallas_call(
        functools.partial(_outproj_kernel, nk=nk),
        grid=(b, t // tm, nk),
        in_specs=[
            pl.BlockSpec((1, tm, tk), lambda bi, i, k: (bi, i, k)),
            pl.BlockSpec((tk, d), lambda bi, i, k: (k, 0)),
            pl.BlockSpec((1, tm, d), lambda bi, i, k: (bi, i, 0)),
            pl.BlockSpec((1, 1, d), lambda bi, i, k: (bi, 0, 0)),
            pl.BlockSpec((1, d), lambda bi, i, k: (0, 0)),
        ],
        out_specs=pl.BlockSpec((1, tm, d), lambda bi, i, k: (bi, i, 0)),
        out_shape=jax.ShapeDtypeStruct((b, t, d), F32),
        scratch_shapes=[pltpu.VMEM((tm, d), F32)],
        compiler_params=_cparams("parallel", "parallel", "arbitrary"),
        name=name,
    )(a, w, h, g, nw.reshape(1, d))


def _dft_kernel(*refs, n_l, n_r, n_acc, terms, n_aux, n_out, epi, nk):
    refs = list(refs)
    l_refs = [refs.pop(0) for _ in range(n_l)]
    r_refs = [refs.pop(0) for _ in range(n_r)]
    aux_refs = [refs.pop(0) for _ in range(n_aux)]
    out_refs = [refs.pop(0) for _ in range(n_out)]
    acc_refs = [refs.pop(0) for _ in range(n_acc)]
    k = pl.program_id(3)
    rs = [r[0].astype(BF16) for r in r_refs]
    parts = [None] * n_acc
    for o, l, r in terms:
        dd = _dot(l_refs[l][...], rs[r])
        parts[o] = dd if parts[o] is None else parts[o] + dd

    @pl.when(k == 0)
    def _():
        for o in range(n_acc):
            acc_refs[o][...] = parts[o]

    @pl.when(k > 0)
    def _():
        for o in range(n_acc):
            acc_refs[o][...] += parts[o]

    @pl.when(k == nk - 1)
    def _():
        epi(pl.program_id(1), [a[...] for a in acc_refs], aux_refs, out_refs)


def _dft_call(l_mats, r_views, terms, n_acc, *, batch, m, kdim, n, tm, tn, tk, epi, out_shapes, out_specs,
              aux=(), aux_specs=(), name):
    tm = _tile(m, tm, 16)
    tn = _tile(n, tn)
    tk = _tile(kdim, tk)
    nk = kdim // tk
    ins, specs = [], []
    for lm in l_mats:
        ins.append(lm)
        specs.append(pl.BlockSpec((tm, tk), lambda b, i, j, k: (i, k)))
    for arr, off, batched in r_views:
        ins.append(arr)
        if batched:
            specs.append(pl.BlockSpec((1, tk, tn), lambda b, i, j, k, off=off: (b, k, off + j)))
        else:
            specs.append(pl.BlockSpec((1, tk, tn), lambda b, i, j, k, off=off: (0, k, off + j)))
    ins += list(aux)
    specs += list(aux_specs(tm, tn))
    kern = functools.partial(_dft_kernel, n_l=len(l_mats), n_r=len(r_views), n_acc=n_acc, terms=terms,
                             n_aux=len(aux), n_out=len(out_shapes), epi=epi, nk=nk)
    return pl.pallas_call(
        kern,
        grid=(batch, m // tm, n // tn, nk),
        in_specs=specs,
        out_specs=out_specs(tm, tn),
        out_shape=out_shapes,
        scratch_shapes=[pltpu.VMEM((tm, tn), F32) for _ in range(n_acc)],
        compiler_params=_cparams("parallel", "parallel", "parallel", "arbitrary"),
        name=name,
    )(*ins)


def _int_grid(n):
    k = lax.broadcasted_iota(jnp.int32, (n, n), 0)
    j = lax.broadcasted_iota(jnp.int32, (n, n), 1)
    return k, j


def _fourier_tables(n, gdim):
    k, j = _int_grid(n)
    ang = ((k * j) % n).astype(F32) * (2.0 * math.pi / n)
    cn, sn = jnp.cos(ang).astype(BF16), (-jnp.sin(ang)).astype(BF16)
    kc, jc = _int_grid(gdim)
    angc = ((kc * jc) % gdim).astype(F32) * (2.0 * math.pi / gdim)
    return cn, sn, jnp.cos(angc), jnp.sin(angc)


def _wfold_kernel(w_ref, c_ref, s_ref, oc_ref, os_ref):
    w = w_ref[...]
    oc_ref[...] = _dot3(w, c_ref[...]).astype(BF16)
    os_ref[...] = _dot3(w, s_ref[...]).astype(BF16)


def _fold_channel_dft(w_in, cc, sc):
    d = w_in.shape[0]
    gdim = cc.shape[0]
    ng = d // gdim
    oc, os_ = pl.pallas_call(
        _wfold_kernel,
        grid=(ng,),
        in_specs=[
            pl.BlockSpec((d, gdim), lambda g: (0, g)),
            pl.BlockSpec((gdim, gdim), lambda g: (0, 0)),
            pl.BlockSpec((gdim, gdim), lambda g: (0, 0)),
        ],
        out_specs=[pl.BlockSpec((d, gdim), lambda g: (0, g)), pl.BlockSpec((d, gdim), lambda g: (0, g))],
        out_shape=[jax.ShapeDtypeStruct((d, d), BF16), jax.ShapeDtypeStruct((d, d), BF16)],
        compiler_params=_cparams("parallel"),
        name="fourier_fold",
    )(w_in, cc, sc)
    return jnp.concatenate([oc, os_], axis=1)


def _fourier_mixer(h, nw, sc, sh, w_fold, tables, w_out_bf, g, nw_post, tag):
    b, t, d = h.shape
    cn, sn = tables
    tn = _tile(2 * d, 512)
    shapes, specs = _std_out(b, t, 2 * d, tn, BF16)
    a = _proj(h, nw, sc, sh, [(w_fold, 0)], n_col_tiles=2 * d // tn, tn=tn, tm=1024, epi=_epi_identity,
              out_shapes=shapes, out_specs=specs, name="fourier_in_" + tag)[0]
    scale = 1.0 / math.sqrt(t * (d // FN_GROUPS))

    def epi(i, accs, aux_refs, out_refs):
        out_refs[0][0] = (accs[0] * scale).astype(BF16)

    tn2 = _tile(d, 1024)
    f = _dft_call(
        [cn, sn], [(a, 0, True), (a, d // tn2, True)], [(0, 0, 0), (0, 1, 1)], 1,
        batch=b, m=t, kdim=t, n=d, tm=1024, tn=tn2, tk=512, epi=epi,
        out_shapes=[jax.ShapeDtypeStruct((b, t, d), BF16)],
        out_specs=lambda tm, tn_: [pl.BlockSpec((1, tm, tn_), lambda bi, i, j, k: (bi, i, j))],
        aux_specs=lambda tm, tn_: [], name="fourier_pos_" + tag)[0]
    return _outproj(f, w_out_bf, h, g, nw_post, name="fourier_out_" + tag)


def _rope_tables(n_tokens):
    rows = n_tokens // GRID_W
    row = jnp.broadcast_to(jnp.arange(rows, dtype=F32)[:, None], (rows, GRID_W)).reshape(n_tokens)
    col = jnp.broadcast_to(jnp.arange(GRID_W, dtype=F32)[None, :], (rows, GRID_W)).reshape(n_tokens)
    inv_freq = ROPE_THETA ** (-jnp.arange(ROPE_FREQS, dtype=F32) / ROPE_FREQS)
    ar, ac = row[:, None] * inv_freq, col[:, None] * inv_freq
    cos = jnp.concatenate([jnp.cos(ar), jnp.cos(ar), jnp.cos(ac), jnp.cos(ac)], axis=1)
    sin = jnp.concatenate([-jnp.sin(ar), jnp.sin(ar), -jnp.sin(ac), jnp.sin(ac)], axis=1)
    return cos, sin


def _attn_in_epi(j, ys, aux_refs, out_refs, *, n_q_tiles, heads_per_tile, rope):
    y = ys[0]
    gain_ref = aux_refs[0]
    o_ref = out_refs[0]

    @pl.when(j <= n_q_tiles)
    def _():
        gain = gain_ref[...] * jnp.where(j < n_q_tiles, ATTN_SCALE, 1.0)
        outs = []
        for hh in range(heads_per_tile):
            t = y[:, hh * HEAD_DIM:(hh + 1) * HEAD_DIM]
            t = t * lax.rsqrt(jnp.mean(t * t, axis=-1, keepdims=True) + RMS_EPS)
            t = t * gain[:, hh * HEAD_DIM:(hh + 1) * HEAD_DIM]
            if rope:
                cos, sin = aux_refs[1][...], aux_refs[2][...]
                lane = lax.broadcasted_iota(jnp.int32, t.shape, 1)
                half = ROPE_FREQS
                partner = jnp.where(lane % (2 * half) < half,
                                    pltpu.roll(t, HEAD_DIM - half, axis=1), pltpu.roll(t, half, axis=1))
                t = t * cos + partner * sin
            outs.append(t)
        o_ref[0] = jnp.concatenate(outs, axis=1).astype(o_ref.dtype)

    @pl.when(j > n_q_tiles)
    def _():
        o_ref[0] = y.astype(o_ref.dtype)


def _attn_project(h, nw, sc, sh, w_in_bf, gains, rope_tabs, tag):
    b, t, d = h.shape
    n = w_in_bf.shape[1]
    tn = N_KV_HEADS * HEAD_DIM
    n_q_tiles = d // tn
    rope = rope_tabs is not None
    aux = [gains]
    aux_specs = [pl.BlockSpec((1, tn), lambda bi, i, j: (0, jnp.minimum(j, n_q_tiles)))]
    tm = _tile(t, 1024, 16)
    if rope:
        aux += list(rope_tabs)
        aux_specs += [pl.BlockSpec((tm, HEAD_DIM), lambda bi, i, j: (i, 0))] * 2
    epi = functools.partial(_attn_in_epi, n_q_tiles=n_q_tiles, heads_per_tile=tn // HEAD_DIM, rope=rope)
    shapes, specs = _std_out(b, t, n, tn, BF16)
    return _proj(h, nw, sc, sh, [(w_in_bf, 0)], n_col_tiles=n // tn, tn=tn, tm=tm, epi=epi,
                 out_shapes=shapes, out_specs=specs, aux=aux, aux_specs=aux_specs, name="attn_in_" + tag)[0]


def _attn_kernel(*refs, has_lat, tkv, n_lat_chunks):
    if has_lat:
        q_ref, kc_ref, vc_ref, k_ref, v_ref, o_ref = refs
    else:
        q_ref, kc_ref, vc_ref, o_ref = refs
    q = q_ref[0]
    s = _dot_nt(q, kc_ref[0])
    m = jnp.max(s, axis=-1, keepdims=True)
    p = jnp.exp(s - m)
    l = jnp.sum(p, axis=-1, keepdims=True)
    acc = _dot(p.astype(BF16), vc_ref[0])
    if has_lat:
        for c in range(n_lat_chunks):
            kk = k_ref[0, c * tkv:(c + 1) * tkv, :]
            vv = v_ref[0, c * tkv:(c + 1) * tkv, :]
            s = _dot_nt(q, kk)
            m_new = jnp.maximum(m, jnp.max(s, axis=-1, keepdims=True))
            alpha = jnp.exp(m - m_new)
            p = jnp.exp(s - m_new)
            l = alpha * l + jnp.sum(p, axis=-1, keepdims=True)
            acc = alpha * acc + _dot(p.astype(BF16), vv)
            m = m_new
    o_ref[0] = (acc / l).astype(o_ref.dtype)


def _attention(qkv, qkv_ctx, d, *, tq=512, tkv=1024):
    has_lat = qkv is not None
    src = qkv if has_lat else qkv_ctx
    b, t, _ = src.shape
    tc = qkv_ctx.shape[1]
    n_heads = d // HEAD_DIM
    grp = n_heads // N_KV_HEADS
    k_off = n_heads
    v_off = n_heads + N_KV_HEADS
    tq = _tile(t, tq, 16)
    tkv = _tile(t, tkv)
    ins = [src, qkv_ctx, qkv_ctx]
    specs = [
        pl.BlockSpec((1, tq, HEAD_DIM), lambda bi, hh, i: (bi, i, hh)),
        pl.BlockSpec((1, tc, HEAD_DIM), lambda bi, hh, i: (bi, 0, k_off + hh // grp)),
        pl.BlockSpec((1, tc, HEAD_DIM), lambda bi, hh, i: (bi, 0, v_off + hh // grp)),
    ]
    if has_lat:
        ins += [qkv, qkv]
        specs += [
            pl.BlockSpec((1, t, HEAD_DIM), lambda bi, hh, i: (bi, 0, k_off + hh // grp)),
            pl.BlockSpec((1, t, HEAD_DIM), lambda bi, hh, i: (bi, 0, v_off + hh // grp)),
        ]
    return pl.pallas_call(
        functools.partial(_attn_kernel, has_lat=has_lat, tkv=tkv, n_lat_chunks=t // tkv),
        grid=(b, n_heads, t // tq),
        in_specs=specs,
        out_specs=pl.BlockSpec((1, tq, HEAD_DIM), lambda bi, hh, i: (bi, i, hh)),
        out_shape=jax.ShapeDtypeStruct((b, t, d), BF16),
        compiler_params=_cparams("parallel", "parallel", "parallel"),
        name="attn_core_lat" if has_lat else "attn_core_ctx",
    )(*ins)


def _dt_epi(j, ys, aux_refs, out_refs):
    x = ys[0] + aux_refs[0][...]
    dt = jnp.maximum(x, 0.0) + jnp.log(1.0 + jnp.exp(-jnp.abs(x)))
    out_refs[0][0] = dt
    out_refs[1][0] = dt * (-jnp.exp(aux_refs[1][...]))


def _ssd_kernel(*refs, rev, with_output, final, nc):
    refs = list(refs)
    x_ref, b_ref, c_ref, dt_ref, adt_ref, s0_ref = [refs.pop(0) for _ in range(6)]
    if final:
        yprev_ref, z_ref, dskip_ref, nw_ref = [refs.pop(0) for _ in range(4)]
    y_ref = refs.pop(0) if with_output else None
    sfin_ref = refs.pop(0)
    st_ref = refs.pop(0)
    c = pl.program_id(2)
    q = SSM_CHUNK
    hp = SSM_HEAD_DIM
    n_hg = dt_ref.shape[2]
    width = n_hg * hp

    @pl.when(c == 0)
    def _():
        st_ref[...] = s0_ref[0, 0]

    ii = lax.broadcasted_iota(jnp.int32, (q, q), 0)
    jj = lax.broadcasted_iota(jnp.int32, (q, q), 1)
    mask = (jj >= ii) if rev else (jj <= ii)
    mask_bf = mask.astype(F32).astype(BF16)
    mask_t_bf = ((ii >= jj) if rev else (ii <= jj)).astype(F32).astype(BF16)
    eye_bf = (ii == jj).astype(F32).astype(BF16)

    dt_r = dt_ref[0, 0]
    a_r = adt_ref[0, 0]

    def hi_lo(v):
        hi = v.astype(BF16).astype(F32)
        return hi, v - hi

    def expand_rows(v):
        return jnp.concatenate([jnp.broadcast_to(v[hh:hh + 1, :], (hp, q)) for hh in range(n_hg)], axis=0)

    a_hi, a_lo = hi_lo(a_r)
    d_hi, d_lo = hi_lo(dt_r)
    e_acs = (_dot_nt(mask_bf, expand_rows(a_hi).astype(BF16)) + _dot_nt(mask_bf, expand_rows(a_lo).astype(BF16)))
    e_dt = (_dot_nt(eye_bf, expand_rows(d_hi).astype(BF16)) + _dot_nt(eye_bf, expand_rows(d_lo).astype(BF16)))
    last = e_acs[0:1, :] if rev else e_acs[q - 1:q, :]

    x = x_ref[0]
    bm = b_ref[0]
    s_old = st_ref[...]
    wend = jnp.exp(last - e_acs) * e_dt
    s_new = s_old * jnp.exp(last) + _dot(bm.T.astype(BF16), (x * wend).astype(BF16))
    st_ref[...] = s_new

    @pl.when(c == nc - 1)
    def _():
        sfin_ref[0, 0] = s_new

    if with_output:
        cm_bf = c_ref[0].astype(BF16)
        cb = _dot_nt(cm_bf, bm.astype(BF16))
        a16 = jnp.concatenate([a_hi, a_lo], axis=0).astype(BF16)
        r16 = _dot(a16, mask_t_bf)
        acs_r = r16[0:n_hg] + r16[n_hg:2 * n_hg]
        y_state = _dot(cm_bf, s_old.astype(BF16)) * jnp.exp(e_acs)
        lane = lax.broadcasted_iota(jnp.int32, (q, 2 * hp), 1)
        ys = []
        for pair in range(n_hg // 2):
            ws = []
            for hh in (2 * pair, 2 * pair + 1):
                seg = e_acs[:, hh * hp:hh * hp + 1] - acs_r[hh:hh + 1, :]
                decay = jnp.exp(jnp.where(mask, seg, NEG_BIG))
                ws.append((cb * decay * dt_r[hh:hh + 1, :]).astype(BF16))
            xp = x[:, pair * 2 * hp:(pair + 1) * 2 * hp]
            bd = jnp.concatenate([jnp.where(lane < hp, xp, 0.0), jnp.where(lane >= hp, xp, 0.0)], axis=0)
            ys.append(_dot(jnp.concatenate(ws, axis=1), bd.astype(BF16)))
        y = jnp.concatenate(ys, axis=1) + y_state
        if final:
            yt = yprev_ref[0] + y + x * dskip_ref[...]
            yt = yt * _silu(z_ref[0])
            y_ref[0] = _rms_rows(yt, nw_ref[...]).astype(y_ref.dtype)
        else:
            y_ref[0] = y


def _ssd_scan(xbc, dt_rows, adt_rows, s0, d_inner, *, rev, with_output, final_args=None):
    b, t, _ = xbc.shape
    g = SSM_GROUPS
    q = SSM_CHUNK
    nc = t // q
    width = d_inner // g
    n_hg = width // SSM_HEAD_DIM
    b_off = d_inner // SSM_STATE
    c_off = b_off + g
    final = final_args is not None
    cmap = (lambda c: nc - 1 - c) if rev else (lambda c: c)
    ins = [xbc, xbc, xbc, dt_rows, adt_rows, s0]
    specs = [
        pl.BlockSpec((1, q, width), lambda bi, gi, c: (bi, cmap(c), gi)),
        pl.BlockSpec((1, q, SSM_STATE), lambda bi, gi, c: (bi, cmap(c), b_off + gi)),
        pl.BlockSpec((1, q, SSM_STATE), lambda bi, gi, c: (bi, cmap(c), c_off + gi)),
        pl.BlockSpec((1, 1, n_hg, q), lambda bi, gi, c: (bi, gi, 0, cmap(c))),
        pl.BlockSpec((1, 1, n_hg, q), lambda bi, gi, c: (bi, gi, 0, cmap(c))),
        pl.BlockSpec((1, 1, SSM_STATE, width), lambda bi, gi, c: (bi, gi, 0, 0)),
    ]
    if final:
        yprev, z, dskip, nw = final_args
        ins += [yprev, z, dskip, nw]
        specs += [
            pl.BlockSpec((1, q, width), lambda bi, gi, c: (bi, cmap(c), gi)),
            pl.BlockSpec((1, q, width), lambda bi, gi, c: (bi, cmap(c), gi)),
            pl.BlockSpec((1, width), lambda bi, gi, c: (0, gi)),
            pl.BlockSpec((1, width), lambda bi, gi, c: (0, gi)),
        ]
    out_shapes, out_specs = [], []
    if with_output:
        out_shapes.append(jax.ShapeDtypeStruct((b, t, d_inner), BF16 if final else F32))
        out_specs.append(pl.BlockSpec((1, q, width), lambda bi, gi, c: (bi, cmap(c), gi)))
    out_shapes.append(jax.ShapeDtypeStruct(s0.shape, F32))
    out_specs.append(pl.BlockSpec((1, 1, SSM_STATE, width), lambda bi, gi, c: (bi, gi, 0, 0)))
    return pl.pallas_call(
        functools.partial(_ssd_kernel, rev=rev, with_output=with_output, final=final, nc=nc),
        grid=(b, g, nc),
        in_specs=specs,
        out_specs=out_specs,
        out_shape=out_shapes,
        scratch_shapes=[pltpu.VMEM((SSM_STATE, width), F32)],
        compiler_params=_cparams("parallel", "parallel", "arbitrary"),
        name="ssd_scan_%s%s" % ("bwd" if rev else "fwd", "" if with_output else "_state"),
    )(*ins)


def _ssd_project(h, nw, sc, sh, wz, wx, wdt, cw, cb, dt_bias, a_log, need_z, tag):
    b, t, d = h.shape
    d_inner = wz.shape[1]
    z = None
    if need_z:
        tn = _tile(d_inner, 512)
        shapes, specs = _std_out(b, t, d_inner, tn, F32)
        z = _proj(h, nw, sc, sh, [(wz, 0)], n_col_tiles=d_inner // tn, tn=tn, tm=1024, epi=_epi_identity,
                  out_shapes=shapes, out_specs=specs, name="ssd_in_z_" + tag)[0]
    nx = wx.shape[1]
    tn = _tile(nx, 512)
    shapes, specs = _std_out(b, t, nx, tn, F32)
    xbc = _proj(h, nw, sc, sh, [(wx, 0)], n_col_tiles=nx // tn, tn=tn, tm=1024, epi=_epi_silu,
                conv=(cw, cb), out_shapes=shapes, out_specs=specs, name="ssd_in_xbc_" + tag)[0]
    nd = wdt.shape[1]
    shapes = [jax.ShapeDtypeStruct((b, t, nd), F32)] * 2
    specs = lambda tm: [pl.BlockSpec((1, tm, nd), lambda bi, i, j: (bi, i, 0))] * 2
    dt, adt = _proj(h, nw, sc, sh, [(wdt, 0)], n_col_tiles=1, tn=nd, tm=1024, epi=_dt_epi,
                    out_shapes=shapes, out_specs=specs,
                    aux=[dt_bias.reshape(1, nd), a_log.reshape(1, nd)],
                    aux_specs=[pl.BlockSpec((1, nd), lambda bi, i, j: (0, 0))] * 2, name="ssd_in_dt_" + tag)

    def rows(v):
        return jnp.transpose(v.reshape(b, t, 2, SSM_GROUPS, -1), (2, 0, 3, 4, 1))

    return z, xbc, rows(dt), rows(adt)


def _ssd_mixer(h, hc, mods, mods_c, nw_pre, p, g, nw_post):
    b, t, d = h.shape
    wz, wx, wdt = p["wz"], p["wx"], p["wdt"]
    d_inner = wz.shape[1]
    sc, sh = mods
    scc, shc = mods_c
    z, xbc, dt_r, adt_r = _ssd_project(h, nw_pre, sc, sh, wz, wx, wdt, p["cw"], p["cb"], p["dt_bias"],
                                       p["a_log"], True, "lat")
    _, xbc_c, dt_rc, adt_rc = _ssd_project(hc, nw_pre, scc, shc, wz, wx, wdt, p["cw"], p["cb"], p["dt_bias"],
                                           p["a_log"], False, "ctx")
    s0 = jnp.zeros((b, SSM_GROUPS, SSM_STATE, d_inner // SSM_GROUPS), F32)
    (s_f,) = _ssd_scan(xbc_c, dt_rc[0], adt_rc[0], s0, d_inner, rev=False, with_output=False)
    (s_b,) = _ssd_scan(xbc_c, dt_rc[1], adt_rc[1], s0, d_inner, rev=True, with_output=False)
    y_f, _ = _ssd_scan(xbc, dt_r[0], adt_r[0], s_f, d_inner, rev=False, with_output=True)
    dskip = jnp.repeat(p["d_skip"], SSM_HEAD_DIM).reshape(1, d_inner)
    yn, _ = _ssd_scan(xbc, dt_r[1], adt_r[1], s_b, d_inner, rev=True, with_output=True,
                      final_args=(y_f, z, dskip, p["norm_w"].reshape(1, d_inner)))
    return _outproj(yn, p["w_out"], h, g, nw_post, name="ssd_out")


def _hyena_tables(n):
    k, j = _int_grid(n)
    ang = ((k * j) % (2 * n)).astype(F32) * (math.pi / n)
    cos, sin = jnp.cos(ang), jnp.sin(ang)
    alt_j = jnp.where(j % 2 == 0, 1.0, -1.0)
    alt_k = jnp.where(k % 2 == 0, 1.0, -1.0)
    fs = jnp.where(k == 0, alt_j, -sin)
    gs = jnp.where(k == 0, alt_j, sin)
    fst = jnp.where(j == 0, alt_k, -sin)
    return cos.astype(BF16), fs.astype(BF16), gs.astype(BF16), fst.astype(BF16)


def _hyena_feats(n):
    t01 = jnp.linspace(0.0, 1.0, n, dtype=F32)[:, None]
    w = (2.0 * math.pi / n) * jnp.arange(n, dtype=F32)[:, None]
    f = jnp.linspace(1e-4, HY_BANDS - 1, HY_BANDS, dtype=F32)[None, :]
    feats = jnp.concatenate([t01, jnp.cos(f * w), -jnp.sin(f * w)], axis=-1)
    return jnp.pad(feats, ((0, 0), (0, LANES - HY_EMB)))


def _taps_kernel(feats_ref, w1_ref, b1_ref, w2_ref, b2_ref, fr_ref, w3_ref, dec_ref, o_ref, hid_ref, *,
                 tiles_per_dir):
    j = pl.program_id(0)

    @pl.when(j == 0)
    def _():
        fr = fr_ref[...]
        h1 = jnp.sin(fr * (_dot3(feats_ref[...], w1_ref[...]) + b1_ref[...]))
        hid_ref[...] = jnp.sin(fr * (_dot3(h1, w2_ref[...]) + b2_ref[...]))

    taps = _dot3(hid_ref[...], w3_ref[...])
    taps = taps * jnp.exp(-feats_ref[:, 0:1] * jnp.abs(dec_ref[...]))
    is_bwd = (j // tiles_per_dir) % 2 == 1
    row = lax.broadcasted_iota(jnp.int32, taps.shape, 0)
    o_ref[0] = jnp.where((row == 0) & is_bwd, 0.0, taps).astype(o_ref.dtype)


def _hyena_taps(n, p, d):
    feats = _hyena_feats(n)
    tn = _tile(d, 256)
    tpd = d // tn
    ncols = HY_ORDER * 2 * d

    def omap(j):
        return ((j // tpd) % 2, 0, (j // (2 * tpd)) * tpd + j % tpd)

    full = lambda j: (0, 0)
    return pl.pallas_call(
        functools.partial(_taps_kernel, tiles_per_dir=tpd),
        grid=(ncols // tn,),
        in_specs=[
            pl.BlockSpec((n, LANES), full),
            pl.BlockSpec((LANES, LANES), full), pl.BlockSpec((1, LANES), full),
            pl.BlockSpec((LANES, LANES), full), pl.BlockSpec((1, LANES), full),
            pl.BlockSpec((1, LANES), full),
            pl.BlockSpec((LANES, tn), lambda j: (0, j)),
            pl.BlockSpec((1, tn), lambda j: (0, j)),
        ],
        out_specs=pl.BlockSpec((1, n, tn), omap),
        out_shape=jax.ShapeDtypeStruct((2, n, HY_ORDER * d), BF16),
        scratch_shapes=[pltpu.VMEM((n, LANES), F32)],
        compiler_params=_cparams("arbitrary"),
        name="hyena_taps",
    )(feats, p["fw1"], p["fb1"], p["fw2"], p["fb2"], p["ffreq"], p["fw3"], p["decay"])


def _hyena_spectrum(taps, tables, n, d):
    cos, fs, gs, _ = tables
    ncol = HY_ORDER * d

    def epi(i, accs, aux_refs, out_refs):
        out_refs[0][...] = accs[0]
        out_refs[1][...] = accs[1]

    kf, kb = taps[0:1], taps[1:2]
    return _dft_call(
        [cos, fs, gs], [(kf, 0, False), (kb, 0, False)],
        [(0, 0, 0), (0, 0, 1), (1, 1, 0), (1, 2, 1)], 2,
        batch=1, m=n, kdim=n, n=ncol, tm=1024, tn=1024, tk=512, epi=epi,
        out_shapes=[jax.ShapeDtypeStruct((n, ncol), F32)] * 2,
        out_specs=lambda tm, tn: [pl.BlockSpec((tm, tn), lambda b, i, j, k: (i, j))] * 2,
        aux_specs=lambda tm, tn: [], name="hyena_spectrum")


def _hyena_long_conv(src, src_off, gate, gate_off, spec_r, spec_i, order, bias, tables, d, out_dtype, tag):
    b, n, _ = src.shape
    cos, fs, _, fst = tables
    inv_n = 1.0 / n

    def fwd_epi(i, accs, aux_refs, out_refs):
        vr, vi = accs
        hr, hi = aux_refs[0][...], aux_refs[1][...]
        row = lax.broadcasted_iota(jnp.int32, vr.shape, 0)
        packed = (row == 0) & (i == 0)
        yr = jnp.where(packed, 0.5 * vr * hr, vr * hr - vi * hi)
        yi = jnp.where(packed, 0.5 * vi * hi, vr * hi + vi * hr)
        out_refs[0][0] = (yr * inv_n).astype(BF16)
        out_refs[1][0] = (yi * inv_n).astype(BF16)

    tn = _tile(d, 1024)
    nj = d // tn
    yr, yi = _dft_call(
        [cos, fs], [(src, src_off * nj, True)], [(0, 0, 0), (1, 1, 0)], 2,
        batch=b, m=n, kdim=n, n=d, tm=1024, tn=tn, tk=512, epi=fwd_epi,
        out_shapes=[jax.ShapeDtypeStruct((b, n, d), BF16)] * 2,
        out_specs=lambda tm, tn_: [pl.BlockSpec((1, tm, tn_), lambda bi, i, j, k: (bi, i, j))] * 2,
        aux=[spec_r, spec_i],
        aux_specs=lambda tm, tn_: [pl.BlockSpec((tm, tn_), lambda bi, i, j, k: (i, order * nj + j))] * 2,
        name="hyena_fwd_" + tag)

    def inv_epi(i, accs, aux_refs, out_refs):
        t = aux_refs[0][0]
        out_refs[0][0] = (aux_refs[1][0] * (accs[0] + t * aux_refs[2][...])).astype(out_dtype)

    return _dft_call(
        [cos, fst], [(yr, 0, True), (yi, 0, True)], [(0, 0, 0), (0, 1, 1)], 1,
        batch=b, m=n, kdim=n, n=d, tm=1024, tn=tn, tk=512, epi=inv_epi,
        out_shapes=[jax.ShapeDtypeStruct((b, n, d), out_dtype)],
        out_specs=lambda tm, tn_: [pl.BlockSpec((1, tm, tn_), lambda bi, i, j, k: (bi, i, j))],
        aux=[src, gate, bias],
        aux_specs=lambda tm, tn_: [
            pl.BlockSpec((1, tm, tn_), lambda bi, i, j, k: (bi, i, src_off * nj + j)),
            pl.BlockSpec((1, tm, tn_), lambda bi, i, j, k: (bi, i, gate_off * nj + j)),
            pl.BlockSpec((1, tn_), lambda bi, i, j, k: (0, j)),
        ],
        name="hyena_inv_" + tag)[0]


def _hyena_mixer(h, mods, nw_pre, p, g, nw_post):
    b, t, d = h.shape
    sc, sh = mods
    tn = _tile(3 * d, 512)
    shapes, specs = _std_out(b, t, 3 * d, tn, F32)
    pj = _proj(h, nw_pre, sc, sh, [(p["w_in"], 0)], n_col_tiles=3 * d // tn, tn=tn, tm=1024, epi=_epi_identity,
               conv=(p["cw"], p["cb"]), out_shapes=shapes, out_specs=specs, name="hyena_in")[0]
    tables = _hyena_tables(t)
    taps = _hyena_taps(t, p, d)
    spec_r, spec_i = _hyena_spectrum(taps, tables, t, d)
    fbias = p["fbias"]
    z = _hyena_long_conv(pj, 0, pj, 1, spec_r, spec_i, 0, fbias[0:1], tables, d, F32, "o1")
    y = _hyena_long_conv(z, 0, pj, 2, spec_r, spec_i, 1, fbias[1:2], tables, d, BF16, "o2")
    return _outproj(y, p["w_out"], h, g, nw_post, name="hyena_out")


def _conv_ffn(h, sc, sh, g, nw_pre, nw_post, w_up_bf, cw, cb, w_down_bf, tag):
    b, t, d = h.shape
    dff = w_down_bf.shape[0]
    tn = _tile(dff, 512)
    nj = dff // tn
    shapes, specs = _std_out(b, t, dff, tn, BF16)
    act = _proj(h, nw_pre, sc, sh, [(w_up_bf, 0), (w_up_bf, nj)], n_col_tiles=nj, tn=tn, tm=1024,
                epi=_epi_swiglu, conv=(cw, cb.reshape(1, -1)), out_shapes=shapes, out_specs=specs,
                name="ffn_up_" + tag)[0]
    return _outproj(act, w_down_bf, h, g, nw_post, name="ffn_down_" + tag)


def kernel(x, c, ctx, c_ctx, ada_w, ada_b, norm_pre_mix, norm_post_mix, norm_pre_ffn, norm_post_ffn, ffn_up, ffn_conv_w, ffn_conv_b, ffn_down, fn_in, fn_out, attn_in, attn_q_gain, attn_k_gain, attn_out, ssm_in, ssm_conv_w, ssm_conv_b, ssm_dt_bias, ssm_a_log, ssm_d, ssm_norm, ssm_out, hy_in, hy_conv_w, hy_conv_b, hy_f_w1, hy_f_b1, hy_f_w2, hy_f_b2, hy_f_w3, hy_f_freq, hy_decay, hy_f_bias, hy_out):
    b, t, d = x.shape
    tc = ctx.shape[1]
    depth = ada_w.shape[0]
    readers = [i for i in range(depth) if i % N_MIXERS in CTX_READER_KINDS]
    last_reader = max(readers) if readers else -1

    n_rows = -(-(b + 1) // 8) * 8
    c_rows = jnp.zeros((n_rows, d), F32).at[:b].set(c).at[b].set(c_ctx)
    mod = _modulation_all(c_rows, ada_w, ada_b).reshape(depth, n_rows, 6, d)

    h, hc = x, ctx
    for i in range(depth):
        kind, j = i % N_MIXERS, i // N_MIXERS
        ctx_live = i <= last_reader
        ctx_next = i < last_reader
        m_lat = [mod[i, :b, s][:, None, :] for s in range(6)]
        sh1, sc1, g1, sh2, sc2, g2 = m_lat
        if ctx_live:
            m_ctx = [jnp.broadcast_to(mod[i, b, s][None, None, :], (b, 1, d)) for s in range(6)]
            csh1, csc1, cg1, csh2, csc2, cg2 = m_ctx
        nw_pre, nw_post = norm_pre_mix[i], norm_post_mix[i]
        if kind == 0:
            gdim = d // FN_GROUPS
            cn, sn, cc, scs = _fourier_tables(t, gdim)
            w_fold = _fold_channel_dft(fn_in[j], cc, scs)
            w_out_bf = fn_out[j].astype(BF16)
            h = _fourier_mixer(h, nw_pre, sc1, sh1, w_fold, (cn, sn), w_out_bf, g1, nw_post, "lat")
            if ctx_next:
                cn_c, sn_c, _, _ = _fourier_tables(tc, gdim)
                hc = _fourier_mixer(hc, nw_pre, csc1, csh1, w_fold, (cn_c, sn_c), w_out_bf, cg1, nw_post, "ctx")
        elif kind == 1:
            w_in_bf = attn_in[j].astype(BF16)
            w_out_bf = attn_out[j].astype(BF16)
            n_q, n_kv = d // HEAD_DIM, N_KV_HEADS
            gains = jnp.concatenate([jnp.tile(attn_q_gain[j], n_q), jnp.tile(attn_k_gain[j], n_kv)]).reshape(1, -1)
            qkv = _attn_project(h, nw_pre, sc1, sh1, w_in_bf, gains, _rope_tables(t), "lat")
            qkv_c = _attn_project(hc, nw_pre, csc1, csh1, w_in_bf, gains, None, "ctx")
            o = _attention(qkv, qkv_c, d)
            h = _outproj(o, w_out_bf, h, g1, nw_post, name="attn_out_lat")
            if ctx_next:
                oc = _attention(None, qkv_c, d)
                hc = _outproj(oc, w_out_bf, hc, cg1, nw_post, name="attn_out_ctx")
        elif kind == 2:
            w = ssm_in[j]
            d_inner = ssm_out.shape[1]
            n_conv = ssm_conv_w.shape[2]
            p = dict(wz=w[:, :d_inner].astype(BF16), wx=w[:, d_inner:d_inner + n_conv].astype(BF16),
                     wdt=w[:, d_inner + n_conv:].astype(BF16), cw=ssm_conv_w[j], cb=ssm_conv_b[j].reshape(1, -1),
                     dt_bias=ssm_dt_bias[j], a_log=ssm_a_log[j], d_skip=ssm_d[j], norm_w=ssm_norm[j],
                     w_out=ssm_out[j].astype(BF16))
            if ctx_next:
                raise NotImplementedError("context output of the SSD mixer is not needed at this depth")
            h = _ssd_mixer(h, hc, (sc1, sh1), (csc1, csh1), nw_pre, p, g1, nw_post)
        else:
            pad_r = LANES - hy_f_w1.shape[1]
            pad_c = LANES - hy_f_w1.shape[2]
            p = dict(w_in=hy_in[j].astype(BF16), cw=hy_conv_w[j], cb=hy_conv_b[j].reshape(1, -1),
                     fw1=jnp.pad(hy_f_w1[j], ((0, pad_r), (0, pad_c))),
                     fb1=jnp.pad(hy_f_b1[j], (0, pad_c)).reshape(1, LANES),
                     fw2=jnp.pad(hy_f_w2[j], ((0, pad_c), (0, pad_c))),
                     fb2=jnp.pad(hy_f_b2[j], (0, pad_c)).reshape(1, LANES),
                     ffreq=jnp.pad(hy_f_freq[j], (0, pad_c)).reshape(1, LANES),
                     fw3=jnp.pad(hy_f_w3[j], ((0, pad_c), (0, 0))), decay=hy_decay[j].reshape(1, -1),
                     fbias=hy_f_bias[j], w_out=hy_out[j].astype(BF16))
            if ctx_next:
                raise NotImplementedError("context output of the Hyena mixer is not needed at this depth")
            h = _hyena_mixer(h, (sc1, sh1), nw_pre, p, g1, nw_post)
        w_up_bf = ffn_up[i].astype(BF16)
        w_down_bf = ffn_down[i].astype(BF16)
        h = _conv_ffn(h, sc2, sh2, g2, norm_pre_ffn[i], norm_post_ffn[i], w_up_bf, ffn_conv_w[i], ffn_conv_b[i],
                      w_down_bf, "lat")
        if ctx_next:
            hc = _conv_ffn(hc, csc2, csh2, cg2, norm_pre_ffn[i], norm_post_ffn[i], w_up_bf, ffn_conv_w[i],
                           ffn_conv_b[i], w_down_bf, "ctx")
    return h
```

```python
import functools
import math

import jax
import jax.numpy as jnp
from jax import lax
from jax.experimental import pallas as pl
from jax.experimental.pallas import tpu as pltpu

F32 = jnp.float32
BF16 = jnp.bfloat16

GRID_W = 64
N_MIXERS = 4
CTX_READER_KINDS = (1, 2)
RMS_EPS = 1e-6
FN_GROUPS = 4
HEAD_DIM = 128
N_KV_HEADS = 4
ROPE_THETA = 10000.0
ROPE_FREQS = HEAD_DIM // 4
ATTN_SCALE = HEAD_DIM ** -0.5
SSM_HEAD_DIM = 64
SSM_GROUPS = 8
SSM_STATE = 128
SSM_CHUNK = 128
SSD_GROUPS_PER_STEP = 4
HY_ORDER = 2
HY_EMB = 33
HY_BANDS = (HY_EMB - 1) // 2

V7X_VMEM_LIMIT_BYTES = 56 * 1024 * 1024
LANES = 128
CONV_HALO = 16
PROJ_ROW_CHUNK = 128
NEG_BIG = -1e30


def _cparams(*sem):
    return pltpu.CompilerParams(dimension_semantics=sem, vmem_limit_bytes=V7X_VMEM_LIMIT_BYTES)


def _tile(n, pref, mult=LANES):
    if n <= pref:
        return n
    t = (pref // mult) * mult
    while t >= mult:
        if n % t == 0:
            return t
        t -= mult
    return n


def _silu(x):
    return x * (1.0 / (1.0 + jnp.exp(-x)))


def _split_bf16(v):
    hi = v.astype(BF16)
    lo = (v - hi.astype(F32)).astype(BF16)
    return hi, lo


def _dot(a, b):
    return jnp.dot(a, b, preferred_element_type=F32)


def _dot_nt(a, b):
    return lax.dot_general(a, b, (((1,), (1,)), ((), ())), preferred_element_type=F32)


def _dot3(a, b):
    ah, al = _split_bf16(a)
    bh, bl = _split_bf16(b)
    return _dot(ah, bh) + _dot(ah, bl) + _dot(al, bh)


def _rms_rows(x, w):
    return x * lax.rsqrt(jnp.mean(x * x, axis=-1, keepdims=True) + RMS_EPS) * w


def _mod_kernel(c_ref, w_ref, b_ref, o_ref):
    a = _silu(c_ref[...]).astype(BF16)
    o_ref[0] = _dot(a, w_ref[0].astype(BF16)) + b_ref[0]


def _modulation_all(c_rows, ada_w, ada_b):
    depth, d, n6 = ada_w.shape
    r = c_rows.shape[0]
    tn = _tile(n6, 1024)
    return pl.pallas_call(
        _mod_kernel,
        grid=(depth, n6 // tn),
        in_specs=[
            pl.BlockSpec((r, d), lambda l, j: (0, 0)),
            pl.BlockSpec((1, d, tn), lambda l, j: (l, 0, j)),
            pl.BlockSpec((1, 1, tn), lambda l, j: (l, 0, j)),
        ],
        out_specs=pl.BlockSpec((1, r, tn), lambda l, j: (l, 0, j)),
        out_shape=jax.ShapeDtypeStruct((depth, r, n6), F32),
        compiler_params=_cparams("parallel", "parallel"),
        name="ada_mod",
    )(c_rows, ada_w, ada_b.reshape(depth, 1, n6))


def _proj_kernel(*refs, n_w, use_conv, n_aux, n_out, epi, tm, n_row_tiles):
    refs = list(refs)
    h_ref = refs.pop(0)
    if use_conv:
        hp_ref = refs.pop(0)
        hn_ref = refs.pop(0)
    nw_ref, sc_ref, sh_ref = refs.pop(0), refs.pop(0), refs.pop(0)
    w_refs = [refs.pop(0) for _ in range(n_w)]
    cw_refs, cb_refs = [], []
    if use_conv:
        for _ in range(n_w):
            cw_refs.append(refs.pop(0))
            cb_refs.append(refs.pop(0))
    aux_refs = [refs.pop(0) for _ in range(n_aux)]
    out_refs = [refs.pop(0) for _ in range(n_out)]
    u_ref = refs.pop(0)
    p_refs = [refs.pop(0) for _ in range(n_w)] if use_conv else None
    i = pl.program_id(1)
    j = pl.program_id(2)
    hl = CONV_HALO
    off = hl if use_conv else 0
    rc = _tile(tm, PROJ_ROW_CHUNK, 16)

    @pl.when(j == 0)
    def _():
        wv = nw_ref[...] * (1.0 + sc_ref[0])
        shv = sh_ref[0]

        def nm(x):
            return x * lax.rsqrt(jnp.mean(x * x, axis=-1, keepdims=True) + RMS_EPS) * wv + shv

        x = h_ref[0]
        inv = lax.rsqrt(jnp.mean(x * x, axis=-1, keepdims=True) + RMS_EPS)
        for r0 in range(0, tm, rc):
            u_ref[off + r0:off + r0 + rc, :] = (h_ref[0, r0:r0 + rc, :] * inv[r0:r0 + rc] * wv + shv).astype(BF16)
        if use_conv:
            u_ref[0:hl, :] = jnp.where(i == 0, 0.0, nm(hp_ref[0])).astype(BF16)
            u_ref[hl + tm:, :] = jnp.where(i == n_row_tiles - 1, 0.0, nm(hn_ref[0])).astype(BF16)

    if not use_conv:
        epi(j, [_dot(u_ref[...], w_refs[k][...]) for k in range(n_w)], aux_refs, out_refs, slice(0, tm))
        return
    for k in range(n_w):
        p_refs[k][...] = _dot(u_ref[...], w_refs[k][...])
    for r0 in range(0, tm, rc):
        ys = []
        for k in range(n_w):
            cw = cw_refs[k][...]
            p_ref = p_refs[k]
            ys.append(cw[0:1] * p_ref[hl - 1 + r0:hl - 1 + r0 + rc, :] + cw[1:2] * p_ref[hl + r0:hl + r0 + rc, :]
                      + cw[2:3] * p_ref[hl + 1 + r0:hl + 1 + r0 + rc, :] + cb_refs[k][...])
        epi(j, ys, aux_refs, out_refs, slice(r0, r0 + rc))


def _proj(h, nw, sc, sh, w_views, *, n_col_tiles, tn, tm, epi, out_shapes, out_specs,
          conv=None, aux=(), aux_specs=(), name):
    b, t, d = h.shape
    tm = _tile(t, tm, 16)
    n_row_tiles = t // tm
    use_conv = conv is not None
    hl = CONV_HALO
    ins = [h]
    specs = [pl.BlockSpec((1, tm, d), lambda bi, i, j: (bi, i, 0))]
    if use_conv:
        r = tm // hl
        last = t // hl - 1
        ins += [h, h]
        specs += [
            pl.BlockSpec((1, hl, d), lambda bi, i, j: (bi, jnp.maximum(i * r - 1, 0), 0)),
            pl.BlockSpec((1, hl, d), lambda bi, i, j: (bi, jnp.minimum((i + 1) * r, last), 0)),
        ]
    ins += [nw.reshape(1, d), sc, sh]
    specs += [
        pl.BlockSpec((1, d), lambda bi, i, j: (0, 0)),
        pl.BlockSpec((1, 1, d), lambda bi, i, j: (bi, 0, 0)),
        pl.BlockSpec((1, 1, d), lambda bi, i, j: (bi, 0, 0)),
    ]
    for w, off in w_views:
        ins.append(w)
        specs.append(pl.BlockSpec((d, tn), lambda bi, i, j, off=off: (0, off + j)))
    if use_conv:
        cw, cb = conv
        for _, off in w_views:
            ins += [cw, cb]
            specs += [
                pl.BlockSpec((cw.shape[0], tn), lambda bi, i, j, off=off: (0, off + j)),
                pl.BlockSpec((1, tn), lambda bi, i, j, off=off: (0, off + j)),
            ]
    ins += list(aux)
    specs += list(aux_specs)
    rows = tm + 2 * hl if use_conv else tm
    scratch = [pltpu.VMEM((rows, d), BF16)]
    if use_conv:
        scratch += [pltpu.VMEM((rows, tn), F32) for _ in w_views]
    kern = functools.partial(_proj_kernel, n_w=len(w_views), use_conv=use_conv, n_aux=len(aux),
                             n_out=len(out_shapes), epi=epi, tm=tm, n_row_tiles=n_row_tiles)
    return pl.pallas_call(
        kern,
        grid=(b, n_row_tiles, n_col_tiles),
        in_specs=specs,
        out_specs=out_specs(tm),
        out_shape=out_shapes,
        scratch_shapes=scratch,
        compiler_params=_cparams("parallel", "parallel", "arbitrary"),
        name=name,
    )(*ins)


def _std_out(b, t, n, tn, dtype):
    shapes = [jax.ShapeDtypeStruct((b, t, n), dtype)]
    specs = lambda tm: [pl.BlockSpec((1, tm, tn), lambda bi, i, j: (bi, i, j))]
    return shapes, specs


def _epi_identity(j, ys, aux_refs, out_refs, rows):
    for o_ref in out_refs:
        o_ref[0, rows, :] = ys[0].astype(o_ref.dtype)


def _epi_silu(j, ys, aux_refs, out_refs, rows):
    out_refs[0][0, rows, :] = _silu(ys[0]).astype(out_refs[0].dtype)


def _epi_swiglu(j, ys, aux_refs, out_refs, rows):
    out_refs[0][0, rows, :] = (_silu(ys[0]) * ys[1]).astype(out_refs[0].dtype)


def _outproj_kernel(a_ref, w_ref, h_ref, g_ref, nw_ref, o_ref, acc_ref, *, nj, tn):
    j = pl.program_id(2)
    acc_ref[j] = _dot(a_ref[0], w_ref[...])

    @pl.when(j == nj - 1)
    def _():
        ss = None
        for jj in range(nj):
            o = acc_ref[jj]
            s = jnp.sum(o * o, axis=-1, keepdims=True)
            ss = s if ss is None else ss + s
        inv = lax.rsqrt(ss * (1.0 / (nj * tn)) + RMS_EPS)
        for jj in range(nj):
            cols = slice(jj * tn, (jj + 1) * tn)
            o_ref[0, :, cols] = h_ref[0, :, cols] + g_ref[0, :, cols] * (acc_ref[jj] * inv * nw_ref[:, cols])


def _outproj(a, w, h, g, nw, *, tm=512, tn=512, name):
    b, t, kdim = a.shape
    d = w.shape[1]
    tm = _tile(t, tm, 16)
    tn = _tile(d, tn)
    nj = d // tn
    return pl.pallas_call(
        functools.partial(_outproj_kernel, nj=nj, tn=tn),
        grid=(b, t // tm, nj),
        in_specs=[
            pl.BlockSpec((1, tm, kdim), lambda bi, i, j: (bi, i, 0)),
            pl.BlockSpec((kdim, tn), lambda bi, i, j: (0, j)),
            pl.BlockSpec((1, tm, d), lambda bi, i, j: (bi, i, 0)),
            pl.BlockSpec((1, 1, d), lambda bi, i, j: (bi, 0, 0)),
            pl.BlockSpec((1, d), lambda bi, i, j: (0, 0)),
        ],
        out_specs=pl.BlockSpec((1, tm, d), lambda bi, i, j: (bi, i, 0)),
        out_shape=jax.ShapeDtypeStruct((b, t, d), F32),
        scratch_shapes=[pltpu.VMEM((nj, tm, tn), F32)],
        compiler_params=_cparams("parallel", "parallel", "arbitrary"),
        name=name,
    )(a, w, h, g, nw.reshape(1, d))


def _dft_kernel(*refs, n_l, n_r, n_acc, terms, n_aux, n_out, epi):
    refs = list(refs)
    l_refs = [refs.pop(0) for _ in range(n_l)]
    r_refs = [refs.pop(0) for _ in range(n_r)]
    aux_refs = [refs.pop(0) for _ in range(n_aux)]
    out_refs = [refs.pop(0) for _ in range(n_out)]
    accs = [None] * n_acc
    for o, l, r in terms:
        dd = _dot(l_refs[l][0], r_refs[r][0])
        accs[o] = dd if accs[o] is None else accs[o] + dd
    epi(pl.program_id(1), accs, aux_refs, out_refs)


def _dft_call(l_tabs, r_views, terms, n_acc, *, n_par, batch, m, kdim, n, tm, tn, epi, out_shapes, out_specs,
              aux=(), aux_specs=None, name):
    tm = _tile(m, tm, 16)
    tn = _tile(n, tn)
    ins, specs = [], []
    for lt in l_tabs:
        ins.append(lt)
        specs.append(pl.BlockSpec((1, tm, kdim), lambda p, i, j, b: (p, i, 0)))
    for arr, lead_fn, col_fn in r_views:
        ins.append(arr)
        specs.append(pl.BlockSpec((1, kdim, tn),
                                  lambda p, i, j, b, lf=lead_fn, cf=col_fn: (lf(p, b), 0, cf(p, j))))
    ins += list(aux)
    if aux_specs is not None:
        specs += list(aux_specs(tm, tn))
    kern = functools.partial(_dft_kernel, n_l=len(l_tabs), n_r=len(r_views), n_acc=n_acc, terms=terms,
                             n_aux=len(aux), n_out=len(out_shapes), epi=epi)
    return pl.pallas_call(
        kern,
        grid=(n_par, m // tm, n // tn, batch),
        in_specs=specs,
        out_specs=out_specs(tm, tn),
        out_shape=out_shapes,
        compiler_params=_cparams("parallel", "parallel", "parallel", "parallel"),
        name=name,
    )(*ins)


def _int_grid(n):
    k = lax.broadcasted_iota(jnp.int32, (n, n), 0)
    j = lax.broadcasted_iota(jnp.int32, (n, n), 1)
    return k, j


def _fourier_pos_tables(n):
    k, j = _int_grid(n // 2)
    tabs = []
    for par in (0, 1):
        ang = ((k * (2 * j + par)) % n).astype(F32) * (2.0 * math.pi / n)
        tabs += [jnp.cos(ang).astype(BF16)[None], (-jnp.sin(ang)).astype(BF16)[None]]
    return tabs


def _fourier_chan_tables(gdim):
    kc, jc = _int_grid(gdim)
    angc = ((kc * jc) % gdim).astype(F32) * (2.0 * math.pi / gdim)
    return jnp.cos(angc), jnp.sin(angc)


def _wfold_kernel(w_ref, c_ref, s_ref, oc_ref, os_ref):
    w = w_ref[...]
    oc_ref[...] = _dot3(w, c_ref[...]).astype(BF16)
    os_ref[...] = _dot3(w, s_ref[...]).astype(BF16)


def _fold_channel_dft(w_in, cc, sc):
    d = w_in.shape[0]
    gdim = cc.shape[0]
    ng = d // gdim
    oc, os_ = pl.pallas_call(
        _wfold_kernel,
        grid=(ng,),
        in_specs=[
            pl.BlockSpec((d, gdim), lambda g: (0, g)),
            pl.BlockSpec((gdim, gdim), lambda g: (0, 0)),
            pl.BlockSpec((gdim, gdim), lambda g: (0, 0)),
        ],
        out_specs=[pl.BlockSpec((d, gdim), lambda g: (0, g)), pl.BlockSpec((d, gdim), lambda g: (0, g))],
        out_shape=[jax.ShapeDtypeStruct((d, d), BF16), jax.ShapeDtypeStruct((d, d), BF16)],
        compiler_params=_cparams("parallel"),
        name="fourier_fold",
    )(w_in, cc, sc)
    return jnp.concatenate([oc, os_], axis=1)


def _fourier_mixer(h, nw, sc, sh, w_fold, tables, w_out_bf, g, nw_post, tag):
    b, t, d = h.shape
    half = t // 2
    tn = _tile(2 * d, 512)
    shapes, specs = _std_out(b, t, 2 * d, tn, BF16)
    a = _proj(h, nw, sc, sh, [(w_fold, 0)], n_col_tiles=2 * d // tn, tn=tn, tm=1024, epi=_epi_identity,
              out_shapes=shapes, out_specs=specs, name="fourier_in_" + tag)[0]
    scale = 1.0 / math.sqrt(t * (d // FN_GROUPS))

    def epi(i, accs, aux_refs, out_refs):
        ev, od = accs
        out_refs[0][0, 0] = ((ev + od) * scale).astype(BF16)
        out_refs[0][0, 1] = ((ev - od) * scale).astype(BF16)

    a2 = a.reshape(b, half, 4 * d)
    tn2 = _tile(d, 512)
    nj = d // tn2
    views = [(a2, lambda p, bi: bi, lambda p, j, q=q: q * nj + j) for q in range(4)]
    f = _dft_call(
        tables, views, [(0, 0, 0), (0, 1, 1), (1, 2, 2), (1, 3, 3)], 2,
        n_par=1, batch=b, m=half, kdim=half, n=d, tm=512, tn=tn2, epi=epi,
        out_shapes=[jax.ShapeDtypeStruct((b, 2, half, d), BF16)],
        out_specs=lambda tm, tn_: [pl.BlockSpec((1, 2, tm, tn_), lambda p, i, j, bi: (bi, 0, i, j))],
        name="fourier_pos_" + tag)[0]
    return _outproj(f.reshape(b, t, d), w_out_bf, h, g, nw_post, name="fourier_out_" + tag)


def _rope_tables(n_tokens):
    rows = n_tokens // GRID_W
    row = jnp.broadcast_to(jnp.arange(rows, dtype=F32)[:, None], (rows, GRID_W)).reshape(n_tokens)
    col = jnp.broadcast_to(jnp.arange(GRID_W, dtype=F32)[None, :], (rows, GRID_W)).reshape(n_tokens)
    inv_freq = ROPE_THETA ** (-jnp.arange(ROPE_FREQS, dtype=F32) / ROPE_FREQS)
    ar, ac = row[:, None] * inv_freq, col[:, None] * inv_freq
    cos = jnp.concatenate([jnp.cos(ar), jnp.cos(ar), jnp.cos(ac), jnp.cos(ac)], axis=1)
    sin = jnp.concatenate([-jnp.sin(ar), jnp.sin(ar), -jnp.sin(ac), jnp.sin(ac)], axis=1)
    return cos, sin


def _attn_in_epi(j, ys, aux_refs, out_refs, rows, *, n_q_tiles, heads_per_tile, rope):
    y = ys[0]
    gain_ref = aux_refs[0]
    o_ref = out_refs[0]

    @pl.when(j <= n_q_tiles)
    def _():
        gain = gain_ref[...] * jnp.where(j < n_q_tiles, ATTN_SCALE, 1.0)
        outs = []
        for hh in range(heads_per_tile):
            t = y[:, hh * HEAD_DIM:(hh + 1) * HEAD_DIM]
            t = t * lax.rsqrt(jnp.mean(t * t, axis=-1, keepdims=True) + RMS_EPS)
            t = t * gain[:, hh * HEAD_DIM:(hh + 1) * HEAD_DIM]
            if rope:
                cos, sin = aux_refs[1][...], aux_refs[2][...]
                lane = lax.broadcasted_iota(jnp.int32, t.shape, 1)
                half = ROPE_FREQS
                partner = jnp.where(lane % (2 * half) < half,
                                    pltpu.roll(t, HEAD_DIM - half, axis=1), pltpu.roll(t, half, axis=1))
                t = t * cos + partner * sin
            outs.append(t)
        o_ref[0] = jnp.concatenate(outs, axis=1).astype(o_ref.dtype)

    @pl.when(j > n_q_tiles)
    def _():
        o_ref[0] = y.astype(o_ref.dtype)


def _attn_project(h, nw, sc, sh, w_in_bf, gains, rope_tabs, tag):
    b, t, d = h.shape
    n = w_in_bf.shape[1]
    tn = N_KV_HEADS * HEAD_DIM
    n_q_tiles = d // tn
    rope = rope_tabs is not None
    aux = [gains]
    aux_specs = [pl.BlockSpec((1, tn), lambda bi, i, j: (0, jnp.minimum(j, n_q_tiles)))]
    tm = _tile(t, 1024, 16)
    if rope:
        aux += list(rope_tabs)
        aux_specs += [pl.BlockSpec((tm, HEAD_DIM), lambda bi, i, j: (i, 0))] * 2
    epi = functools.partial(_attn_in_epi, n_q_tiles=n_q_tiles, heads_per_tile=tn // HEAD_DIM, rope=rope)
    shapes, specs = _std_out(b, t, n, tn, BF16)
    return _proj(h, nw, sc, sh, [(w_in_bf, 0)], n_col_tiles=n // tn, tn=tn, tm=tm, epi=epi,
                 out_shapes=shapes, out_specs=specs, aux=aux, aux_specs=aux_specs, name="attn_in_" + tag)[0]


def _attn_kernel(*refs, has_lat, tkv, n_lat_chunks):
    if has_lat:
        q_ref, kc_ref, vc_ref, k_ref, v_ref, o_ref = refs
    else:
        q_ref, kc_ref, vc_ref, o_ref = refs
    q = q_ref[0]
    s = _dot_nt(q, kc_ref[0])
    m = jnp.max(s, axis=-1, keepdims=True)
    p = jnp.exp(s - m)
    l = jnp.sum(p, axis=-1, keepdims=True)
    acc = _dot(p.astype(BF16), vc_ref[0])
    if has_lat:
        for c in range(n_lat_chunks):
            kk = k_ref[0, c * tkv:(c + 1) * tkv, :]
            vv = v_ref[0, c * tkv:(c + 1) * tkv, :]
            s = _dot_nt(q, kk)
            m_new = jnp.maximum(m, jnp.max(s, axis=-1, keepdims=True))
            alpha = jnp.exp(m - m_new)
            p = jnp.exp(s - m_new)
            l = alpha * l + jnp.sum(p, axis=-1, keepdims=True)
            acc = alpha * acc + _dot(p.astype(BF16), vv)
            m = m_new
    o_ref[0] = (acc / l).astype(o_ref.dtype)


def _attention(qkv, qkv_ctx, d, *, tq=512, tkv=1024):
    has_lat = qkv is not None
    src = qkv if has_lat else qkv_ctx
    b, t, _ = src.shape
    tc = qkv_ctx.shape[1]
    n_heads = d // HEAD_DIM
    grp = n_heads // N_KV_HEADS
    k_off = n_heads
    v_off = n_heads + N_KV_HEADS
    tq = _tile(t, tq, 16)
    tkv = _tile(t, tkv)
    ins = [src, qkv_ctx, qkv_ctx]
    specs = [
        pl.BlockSpec((1, tq, HEAD_DIM), lambda bi, hh, i: (bi, i, hh)),
        pl.BlockSpec((1, tc, HEAD_DIM), lambda bi, hh, i: (bi, 0, k_off + hh // grp)),
        pl.BlockSpec((1, tc, HEAD_DIM), lambda bi, hh, i: (bi, 0, v_off + hh // grp)),
    ]
    if has_lat:
        ins += [qkv, qkv]
        specs += [
            pl.BlockSpec((1, t, HEAD_DIM), lambda bi, hh, i: (bi, 0, k_off + hh // grp)),
            pl.BlockSpec((1, t, HEAD_DIM), lambda bi, hh, i: (bi, 0, v_off + hh // grp)),
        ]
    return pl.pallas_call(
        functools.partial(_attn_kernel, has_lat=has_lat, tkv=tkv, n_lat_chunks=t // tkv),
        grid=(b, n_heads, t // tq),
        in_specs=specs,
        out_specs=pl.BlockSpec((1, tq, HEAD_DIM), lambda bi, hh, i: (bi, i, hh)),
        out_shape=jax.ShapeDtypeStruct((b, t, d), BF16),
        compiler_params=_cparams("parallel", "parallel", "parallel"),
        name="attn_core_lat" if has_lat else "attn_core_ctx",
    )(*ins)


def _dt_epi(j, ys, aux_refs, out_refs, rows):
    x = ys[0] + aux_refs[0][...]
    dt = jnp.maximum(x, 0.0) + jnp.log(1.0 + jnp.exp(-jnp.abs(x)))
    out_refs[0][0] = dt
    out_refs[1][0] = dt * (-jnp.exp(aux_refs[1][...]))


def _ssd_kernel(*refs, rev, with_output, final, nc):
    refs = list(refs)
    x_ref, b_ref, c_ref, dt_ref, adt_ref, s0_ref = [refs.pop(0) for _ in range(6)]
    if final:
        yprev_ref, z_ref, dskip_ref, nw_ref = [refs.pop(0) for _ in range(4)]
    y_ref = refs.pop(0) if with_output else None
    sfin_ref = refs.pop(0)
    st_ref = refs.pop(0)
    c = pl.program_id(2)
    q = SSM_CHUNK
    hp = SSM_HEAD_DIM
    ns = SSM_STATE
    gpb = dt_ref.shape[1]
    n_hg = dt_ref.shape[2]
    width = n_hg * hp

    @pl.when(c == 0)
    def _():
        st_ref[...] = s0_ref[0]

    ii = lax.broadcasted_iota(jnp.int32, (q, q), 0)
    jj = lax.broadcasted_iota(jnp.int32, (q, q), 1)
    mask = (jj >= ii) if rev else (jj <= ii)
    mask_bf = mask.astype(F32).astype(BF16)
    mask_t_bf = ((ii >= jj) if rev else (ii <= jj)).astype(F32).astype(BF16)
    eye_bf = (ii == jj).astype(F32).astype(BF16)
    lane = lax.broadcasted_iota(jnp.int32, (q, 2 * hp), 1)

    def hi_lo(v):
        hi = v.astype(BF16).astype(F32)
        return hi, v - hi

    def expand_rows(v):
        return jnp.concatenate([jnp.broadcast_to(v[hh:hh + 1, :], (hp, q)) for hh in range(n_hg)], axis=0)

    for gi in range(gpb):
        cols = slice(gi * width, (gi + 1) * width)
        scols = slice(gi * ns, (gi + 1) * ns)
        dt_r = dt_ref[0, gi]
        a_r = adt_ref[0, gi]
        a_hi, a_lo = hi_lo(a_r)
        d_hi, d_lo = hi_lo(dt_r)
        e_acs = (_dot_nt(mask_bf, expand_rows(a_hi).astype(BF16))
                 + _dot_nt(mask_bf, expand_rows(a_lo).astype(BF16)))
        e_dt = (_dot_nt(eye_bf, expand_rows(d_hi).astype(BF16)) + _dot_nt(eye_bf, expand_rows(d_lo).astype(BF16)))
        last = e_acs[0:1, :] if rev else e_acs[q - 1:q, :]

        x = x_ref[0, :, cols]
        bm = b_ref[0, :, scols]
        s_old = st_ref[gi]
        wend = jnp.exp(last - e_acs) * e_dt
        s_new = s_old * jnp.exp(last) + _dot(bm.T.astype(BF16), (x * wend).astype(BF16))
        st_ref[gi] = s_new

        @pl.when(c == nc - 1)
        def _():
            sfin_ref[0, gi] = s_new

        if with_output:
            cm_bf = c_ref[0, :, scols].astype(BF16)
            cb = _dot_nt(cm_bf, bm.astype(BF16))
            a16 = jnp.concatenate([a_hi, a_lo], axis=0).astype(BF16)
            r16 = _dot(a16, mask_t_bf)
            acs_r = r16[0:n_hg] + r16[n_hg:2 * n_hg]
            y_state = _dot(cm_bf, s_old.astype(BF16)) * jnp.exp(e_acs)
            ys = []
            for pair in range(n_hg // 2):
                ws = []
                for hh in (2 * pair, 2 * pair + 1):
                    seg = e_acs[:, hh * hp:hh * hp + 1] - acs_r[hh:hh + 1, :]
                    decay = jnp.exp(jnp.where(mask, seg, NEG_BIG))
                    ws.append((cb * decay * dt_r[hh:hh + 1, :]).astype(BF16))
                xp = x[:, pair * 2 * hp:(pair + 1) * 2 * hp]
                bd = jnp.concatenate([jnp.where(lane < hp, xp, 0.0), jnp.where(lane >= hp, xp, 0.0)], axis=0)
                ys.append(_dot(jnp.concatenate(ws, axis=1), bd.astype(BF16)))
            y = jnp.concatenate(ys, axis=1) + y_state
            if final:
                yt = yprev_ref[0, :, cols] + y + x * dskip_ref[:, cols]
                yt = yt * _silu(z_ref[0, :, cols])
                y_ref[0, :, cols] = _rms_rows(yt, nw_ref[:, cols]).astype(y_ref.dtype)
            else:
                y_ref[0, :, cols] = y


def _ssd_scan(xbc, dt_rows, adt_rows, s0, d_inner, *, rev, with_output, final_args=None):
    b, t, _ = xbc.shape
    g = SSM_GROUPS
    q = SSM_CHUNK
    nc = t // q
    width = d_inner // g
    n_hg = width // SSM_HEAD_DIM
    gpb = SSD_GROUPS_PER_STEP
    ngb = g // gpb
    wblk = gpb * width
    sblk = gpb * SSM_STATE
    b_off = d_inner // sblk
    c_off = b_off + ngb
    final = final_args is not None
    cmap = (lambda c: nc - 1 - c) if rev else (lambda c: c)
    ins = [xbc, xbc, xbc, dt_rows, adt_rows, s0]
    specs = [
        pl.BlockSpec((1, q, wblk), lambda bi, gi, c: (bi, cmap(c), gi)),
        pl.BlockSpec((1, q, sblk), lambda bi, gi, c: (bi, cmap(c), b_off + gi)),
        pl.BlockSpec((1, q, sblk), lambda bi, gi, c: (bi, cmap(c), c_off + gi)),
        pl.BlockSpec((1, gpb, n_hg, q), lambda bi, gi, c: (bi, gi, 0, cmap(c))),
        pl.BlockSpec((1, gpb, n_hg, q), lambda bi, gi, c: (bi, gi, 0, cmap(c))),
        pl.BlockSpec((1, gpb, SSM_STATE, width), lambda bi, gi, c: (bi, gi, 0, 0)),
    ]
    if final:
        yprev, z, dskip, nw = final_args
        ins += [yprev, z, dskip, nw]
        specs += [
            pl.BlockSpec((1, q, wblk), lambda bi, gi, c: (bi, cmap(c), gi)),
            pl.BlockSpec((1, q, wblk), lambda bi, gi, c: (bi, cmap(c), gi)),
            pl.BlockSpec((1, wblk), lambda bi, gi, c: (0, gi)),
            pl.BlockSpec((1, wblk), lambda bi, gi, c: (0, gi)),
        ]
    out_shapes, out_specs = [], []
    if with_output:
        out_shapes.append(jax.ShapeDtypeStruct((b, t, d_inner), BF16 if final else F32))
        out_specs.append(pl.BlockSpec((1, q, wblk), lambda bi, gi, c: (bi, cmap(c), gi)))
    out_shapes.append(jax.ShapeDtypeStruct(s0.shape, F32))
    out_specs.append(pl.BlockSpec((1, gpb, SSM_STATE, width), lambda bi, gi, c: (bi, gi, 0, 0)))
    return pl.pallas_call(
        functools.partial(_ssd_kernel, rev=rev, with_output=with_output, final=final, nc=nc),
        grid=(b, ngb, nc),
        in_specs=specs,
        out_specs=out_specs,
        out_shape=out_shapes,
        scratch_shapes=[pltpu.VMEM((gpb, SSM_STATE, width), F32)],
        compiler_params=_cparams("parallel", "parallel", "arbitrary"),
        name="ssd_scan_%s%s" % ("bwd" if rev else "fwd", "" if with_output else "_state"),
    )(*ins)


def _ssd_project(h, nw, sc, sh, wz, wx, wdt, cw, cb, dt_bias, a_log, need_z, tag):
    b, t, d = h.shape
    d_inner = wz.shape[1]
    z = None
    if need_z:
        tn = _tile(d_inner, 512)
        shapes, specs = _std_out(b, t, d_inner, tn, F32)
        z = _proj(h, nw, sc, sh, [(wz, 0)], n_col_tiles=d_inner // tn, tn=tn, tm=1024, epi=_epi_identity,
                  out_shapes=shapes, out_specs=specs, name="ssd_in_z_" + tag)[0]
    nx = wx.shape[1]
    tn = _tile(nx, 512)
    shapes, specs = _std_out(b, t, nx, tn, F32)
    xbc = _proj(h, nw, sc, sh, [(wx, 0)], n_col_tiles=nx // tn, tn=tn, tm=1024, epi=_epi_silu,
                conv=(cw, cb), out_shapes=shapes, out_specs=specs, name="ssd_in_xbc_" + tag)[0]
    nd = wdt.shape[1]
    shapes = [jax.ShapeDtypeStruct((b, t, nd), F32)] * 2
    specs = lambda tm: [pl.BlockSpec((1, tm, nd), lambda bi, i, j: (bi, i, 0))] * 2
    dt, adt = _proj(h, nw, sc, sh, [(wdt, 0)], n_col_tiles=1, tn=nd, tm=1024, epi=_dt_epi,
                    out_shapes=shapes, out_specs=specs,
                    aux=[dt_bias.reshape(1, nd), a_log.reshape(1, nd)],
                    aux_specs=[pl.BlockSpec((1, nd), lambda bi, i, j: (0, 0))] * 2, name="ssd_in_dt_" + tag)

    def rows(v):
        return jnp.transpose(v.reshape(b, t, 2, SSM_GROUPS, -1), (2, 0, 3, 4, 1))

    return z, xbc, rows(dt), rows(adt)


def _ssd_mixer(h, hc, mods, mods_c, nw_pre, p, g, nw_post):
    b, t, d = h.shape
    wz, wx, wdt = p["wz"], p["wx"], p["wdt"]
    d_inner = wz.shape[1]
    sc, sh = mods
    scc, shc = mods_c
    z, xbc, dt_r, adt_r = _ssd_project(h, nw_pre, sc, sh, wz, wx, wdt, p["cw"], p["cb"], p["dt_bias"],
                                       p["a_log"], True, "lat")
    _, xbc_c, dt_rc, adt_rc = _ssd_project(hc, nw_pre, scc, shc, wz, wx, wdt, p["cw"], p["cb"], p["dt_bias"],
                                           p["a_log"], False, "ctx")
    s0 = jnp.zeros((b, SSM_GROUPS, SSM_STATE, d_inner // SSM_GROUPS), F32)
    (s_f,) = _ssd_scan(xbc_c, dt_rc[0], adt_rc[0], s0, d_inner, rev=False, with_output=False)
    (s_b,) = _ssd_scan(xbc_c, dt_rc[1], adt_rc[1], s0, d_inner, rev=True, with_output=False)
    y_f, _ = _ssd_scan(xbc, dt_r[0], adt_r[0], s_f, d_inner, rev=False, with_output=True)
    dskip = jnp.repeat(p["d_skip"], SSM_HEAD_DIM).reshape(1, d_inner)
    yn, _ = _ssd_scan(xbc, dt_r[1], adt_r[1], s_b, d_inner, rev=True, with_output=True,
                      final_args=(y_f, z, dskip, p["norm_w"].reshape(1, d_inner)))
    return _outproj(yn, p["w_out"], h, g, nw_post, name="ssd_out")


def _hyena_tables(n):
    half = n // 2
    k, s = _int_grid(half)
    def trig(m):
        ang = (m % (2 * n)).astype(F32) * (math.pi / n)
        return jnp.cos(ang), jnp.sin(ang)
    ce, se = trig(2 * s * k)
    co, so = trig((2 * s + 1) * k)
    cot, sot = trig((2 * k + 1) * s)
    alt_s = jnp.where(s % 2 == 0, 1.0, -1.0)
    alt_k = jnp.where(k % 2 == 0, 1.0, -1.0)
    sep = jnp.where(k == 0, alt_s, se)
    sop = jnp.where(k == 0, alt_s, so)
    g = jnp.where(k == 0, alt_s, -se)
    nsept = jnp.where(s == 0, -alt_k, -se)
    nsopt = jnp.where(s == 0, -alt_k, -sot)
    bf = lambda *ts: jnp.stack([t.astype(BF16) for t in ts])
    return dict(ce=bf(ce), co=bf(co), sep=bf(sep), sop=bf(sop), g=bf(g), nsop=bf(-sop),
                inv_c=bf(ce, cot), inv_s=bf(nsept, nsopt))


def _hyena_feats(n):
    t01 = jnp.linspace(0.0, 1.0, n, dtype=F32)[:, None]
    w = (2.0 * math.pi / n) * jnp.arange(n, dtype=F32)[:, None]
    f = jnp.linspace(1e-4, HY_BANDS - 1, HY_BANDS, dtype=F32)[None, :]
    feats = jnp.concatenate([t01, jnp.cos(f * w), -jnp.sin(f * w)], axis=-1)
    return jnp.pad(feats, ((0, 0), (0, LANES - HY_EMB)))


def _taps_kernel(feats_ref, w1_ref, b1_ref, w2_ref, b2_ref, fr_ref, w3_ref, dec_ref, o_ref, hid_ref, *,
                 tiles_per_dir):
    j = pl.program_id(0)

    @pl.when(j == 0)
    def _():
        fr = fr_ref[...]
        h1 = jnp.sin(fr * (_dot3(feats_ref[...], w1_ref[...]) + b1_ref[...]))
        hid_ref[...] = jnp.sin(fr * (_dot3(h1, w2_ref[...]) + b2_ref[...]))

    taps = _dot3(hid_ref[...], w3_ref[...])
    taps = taps * jnp.exp(-feats_ref[:, 0:1] * jnp.abs(dec_ref[...]))
    is_bwd = (j // tiles_per_dir) % 2 == 1
    row = lax.broadcasted_iota(jnp.int32, taps.shape, 0)
    o_ref[0] = jnp.where((row == 0) & is_bwd, 0.0, taps).astype(o_ref.dtype)


def _hyena_taps(n, p, d):
    feats = _hyena_feats(n)
    tn = _tile(d, 256)
    tpd = d // tn
    ncols = HY_ORDER * 2 * d

    def omap(j):
        return ((j // tpd) % 2, 0, (j // (2 * tpd)) * tpd + j % tpd)

    full = lambda j: (0, 0)
    return pl.pallas_call(
        functools.partial(_taps_kernel, tiles_per_dir=tpd),
        grid=(ncols // tn,),
        in_specs=[
            pl.BlockSpec((n, LANES), full),
            pl.BlockSpec((LANES, LANES), full), pl.BlockSpec((1, LANES), full),
            pl.BlockSpec((LANES, LANES), full), pl.BlockSpec((1, LANES), full),
            pl.BlockSpec((1, LANES), full),
            pl.BlockSpec((LANES, tn), lambda j: (0, j)),
            pl.BlockSpec((1, tn), lambda j: (0, j)),
        ],
        out_specs=pl.BlockSpec((1, n, tn), omap),
        out_shape=jax.ShapeDtypeStruct((2, n, HY_ORDER * d), BF16),
        scratch_shapes=[pltpu.VMEM((n, LANES), F32)],
        compiler_params=_cparams("arbitrary"),
        name="hyena_taps",
    )(feats, p["fw1"], p["fb1"], p["fw2"], p["fb2"], p["ffreq"], p["fw3"], p["decay"])


def _first_row(shape, i):
    return (lax.broadcasted_iota(jnp.int32, shape, 0) == 0) & (i == 0)


def _hyena_spectrum(taps, tabs, n, d):
    half = n // 2
    ncol = HY_ORDER * d
    taps2 = taps.reshape(2, half, 2 * ncol)
    tn = _tile(ncol, 512)
    nj = ncol // tn
    views = [(taps2, lambda p, b, dr=dr: dr, lambda p, j, par=par: par * nj + j)
             for dr in (0, 1) for par in (0, 1)]
    l_tabs = [tabs["ce"], tabs["co"], tabs["sep"], tabs["sop"], tabs["g"], tabs["nsop"]]
    terms = [(0, 0, 0), (0, 0, 2), (1, 1, 1), (1, 1, 3), (2, 2, 0), (2, 4, 2), (3, 3, 1), (3, 5, 3)]

    def epi(i, accs, aux_refs, out_refs):
        pec, poc, pes, pos = accs
        first = _first_row(pec.shape, i)
        out_refs[0][...] = pec + poc
        out_refs[1][...] = jnp.where(first, pes, -(pes + pos))
        out_refs[2][...] = pec - poc
        out_refs[3][...] = jnp.where(first, -pos, pes - pos)

    return _dft_call(
        l_tabs, views, terms, 4, n_par=1, batch=1, m=half, kdim=half, n=ncol, tm=256, tn=tn, epi=epi,
        out_shapes=[jax.ShapeDtypeStruct((half, ncol), F32)] * 4,
        out_specs=lambda tm, tn_: [pl.BlockSpec((tm, tn_), lambda p, i, j, b: (i, j))] * 4,
        name="hyena_spectrum")


def _hyena_long_conv(src_bf, src_f32, src_width, gate, gate_off, spec, order, bias, tabs, d, out_dtypes, tag):
    b, half, _ = src_bf.shape
    n = 2 * half
    inv_n = 1.0 / n
    tn = _tile(d, 512)
    nj = d // tn

    def fwd_epi(i, accs, aux_refs, out_refs):
        pec, poc, pes, pos = accs
        hr, hi, hur, hui = [a[...] for a in aux_refs]
        vr, ur = pec + poc, pec - poc
        vi, ui = -(pes + pos), pes - pos
        yr, yi = vr * hr - vi * hi, vr * hi + vi * hr
        yur, yui = ur * hur - ui * hui, ur * hui + ui * hur
        first = _first_row(pec.shape, i)
        y0, yny = vr * hr, ur * hur
        mr, mi = pes, -pos
        ymr, ymi = mr * hi - mi * hui, mr * hui + mi * hi
        ae = jnp.where(first, 0.5 * (y0 + yny), yr + yur)
        be = jnp.where(first, -ymr, yi - yui)
        ao = jnp.where(first, 0.5 * (y0 - yny), yr - yur)
        bo = jnp.where(first, ymi, yi + yui)
        for q, v in enumerate((ae, be, ao, bo)):
            out_refs[0][0, :, q * tn:(q + 1) * tn] = (v * inv_n).astype(BF16)

    nsrc = src_width // tn
    views = [(src_bf, lambda p, bi: bi, lambda p, j: j), (src_bf, lambda p, bi: bi, lambda p, j: nsrc + j)]
    yab = _dft_call(
        [tabs["ce"], tabs["co"], tabs["sep"], tabs["sop"]], views,
        [(0, 0, 0), (1, 1, 1), (2, 2, 0), (3, 3, 1)], 4,
        n_par=1, batch=b, m=half, kdim=half, n=d, tm=512, tn=tn, epi=fwd_epi,
        out_shapes=[jax.ShapeDtypeStruct((b, half, 4 * d), BF16)],
        out_specs=lambda tm, tn_: [pl.BlockSpec((1, tm, 4 * tn_), lambda p, i, j, bi: (bi, i, j))],
        aux=list(spec),
        aux_specs=lambda tm, tn_: [pl.BlockSpec((tm, tn_), lambda p, i, j, bi: (i, order * nj + j))] * 4,
        name="hyena_fwd_" + tag)[0]

    def inv_epi(i, accs, aux_refs, out_refs):
        val = aux_refs[1][0] * (accs[0] + aux_refs[0][0] * aux_refs[2][...])
        for o_ref in out_refs:
            o_ref[0] = val.astype(o_ref.dtype)

    ngate = 3 * d // tn
    return _dft_call(
        [tabs["inv_c"], tabs["inv_s"]],
        [(yab, lambda p, bi: bi, lambda p, j: 4 * j + 2 * p), (yab, lambda p, bi: bi, lambda p, j: 4 * j + 2 * p + 1)],
        [(0, 0, 0), (0, 1, 1)], 1,
        n_par=2, batch=b, m=half, kdim=half, n=d, tm=1024, tn=tn, epi=inv_epi,
        out_shapes=[jax.ShapeDtypeStruct((b, half, 2 * d), dt) for dt in out_dtypes],
        out_specs=lambda tm, tn_: [pl.BlockSpec((1, tm, tn_), lambda p, i, j, bi: (bi, i, p * nj + j))
                                   for _ in out_dtypes],
        aux=[src_f32, gate, bias],
        aux_specs=lambda tm, tn_: [
            pl.BlockSpec((1, tm, tn_), lambda p, i, j, bi: (bi, i, p * nsrc + j)),
            pl.BlockSpec((1, tm, tn_), lambda p, i, j, bi: (bi, i, p * ngate + gate_off * nj + j)),
            pl.BlockSpec((1, tn_), lambda p, i, j, bi: (0, j)),
        ],
        name="hyena_inv_" + tag)


def _hyena_mixer(h, mods, nw_pre, p, g, nw_post):
    b, t, d = h.shape
    sc, sh = mods
    tn = _tile(3 * d, 512)
    half = t // 2
    shapes = [jax.ShapeDtypeStruct((b, t, 3 * d), F32), jax.ShapeDtypeStruct((b, t, 3 * d), BF16)]
    specs = lambda tm: [pl.BlockSpec((1, tm, tn), lambda bi, i, j: (bi, i, j))] * 2
    pj, pj_bf = _proj(h, nw_pre, sc, sh, [(p["w_in"], 0)], n_col_tiles=3 * d // tn, tn=tn, tm=1024,
                      epi=_epi_identity, conv=(p["cw"], p["cb"]), out_shapes=shapes, out_specs=specs,
                      name="hyena_in")
    tabs = _hyena_tables(t)
    taps = _hyena_taps(t, p, d)
    spec = _hyena_spectrum(taps, tabs, t, d)
    fbias = p["fbias"]
    pj2 = pj.reshape(b, half, 6 * d)
    z, z_bf = _hyena_long_conv(pj_bf.reshape(b, half, 6 * d), pj2, 3 * d, pj2, 1, spec, 0, fbias[0:1], tabs, d,
                               (F32, BF16), "o1")
    (y,) = _hyena_long_conv(z_bf, z, d, pj2, 2, spec, 1, fbias[1:2], tabs, d, (BF16,), "o2")
    return _outproj(y.reshape(b, t, d), p["w_out"], h, g, nw_post, name="hyena_out")


def _conv_ffn(h, sc, sh, g, nw_pre, nw_post, w_up_bf, cw, cb, w_down_bf, tag):
    b, t, d = h.shape
    dff = w_down_bf.shape[0]
    tn = _tile(dff, 512)
    nj = dff // tn
    shapes, specs = _std_out(b, t, dff, tn, BF16)
    act = _proj(h, nw_pre, sc, sh, [(w_up_bf, 0), (w_up_bf, nj)], n_col_tiles=nj, tn=tn, tm=1024,
                epi=_epi_swiglu, conv=(cw, cb.reshape(1, -1)), out_shapes=shapes, out_specs=specs,
                name="ffn_up_" + tag)[0]
    return _outproj(act, w_down_bf, h, g, nw_post, name="ffn_down_" + tag)


def kernel(x, c, ctx, c_ctx, ada_w, ada_b, norm_pre_mix, norm_post_mix, norm_pre_ffn, norm_post_ffn, ffn_up, ffn_conv_w, ffn_conv_b, ffn_down, fn_in, fn_out, attn_in, attn_q_gain, attn_k_gain, attn_out, ssm_in, ssm_conv_w, ssm_conv_b, ssm_dt_bias, ssm_a_log, ssm_d, ssm_norm, ssm_out, hy_in, hy_conv_w, hy_conv_b, hy_f_w1, hy_f_b1, hy_f_w2, hy_f_b2, hy_f_w3, hy_f_freq, hy_decay, hy_f_bias, hy_out):
    b, t, d = x.shape
    tc = ctx.shape[1]
    depth = ada_w.shape[0]
    readers = [i for i in range(depth) if i % N_MIXERS in CTX_READER_KINDS]
    last_reader = max(readers) if readers else -1

    n_rows = -(-(b + 1) // 8) * 8
    c_rows = jnp.zeros((n_rows, d), F32).at[:b].set(c).at[b].set(c_ctx)
    mod = _modulation_all(c_rows, ada_w, ada_b).reshape(depth, n_rows, 6, d)

    h, hc = x, ctx
    for i in range(depth):
        kind, j = i % N_MIXERS, i // N_MIXERS
        ctx_live = i <= last_reader
        ctx_next = i < last_reader
        m_lat = [mod[i, :b, s][:, None, :] for s in range(6)]
        sh1, sc1, g1, sh2, sc2, g2 = m_lat
        if ctx_live:
            m_ctx = [jnp.broadcast_to(mod[i, b, s][None, None, :], (b, 1, d)) for s in range(6)]
            csh1, csc1, cg1, csh2, csc2, cg2 = m_ctx
        nw_pre, nw_post = norm_pre_mix[i], norm_post_mix[i]
        if kind == 0:
            gdim = d // FN_GROUPS
            cc, scs = _fourier_chan_tables(gdim)
            w_fold = _fold_channel_dft(fn_in[j], cc, scs)
            w_out_bf = fn_out[j].astype(BF16)
            h = _fourier_mixer(h, nw_pre, sc1, sh1, w_fold, _fourier_pos_tables(t), w_out_bf, g1, nw_post, "lat")
            if ctx_next:
                hc = _fourier_mixer(hc, nw_pre, csc1, csh1, w_fold, _fourier_pos_tables(tc), w_out_bf, cg1,
                                    nw_post, "ctx")
        elif kind == 1:
            w_in_bf = attn_in[j].astype(BF16)
            w_out_bf = attn_out[j].astype(BF16)
            n_q, n_kv = d // HEAD_DIM, N_KV_HEADS
            gains = jnp.concatenate([jnp.tile(attn_q_gain[j], n_q), jnp.tile(attn_k_gain[j], n_kv)]).reshape(1, -1)
            qkv = _attn_project(h, nw_pre, sc1, sh1, w_in_bf, gains, _rope_tables(t), "lat")
            qkv_c = _attn_project(hc, nw_pre, csc1, csh1, w_in_bf, gains, None, "ctx")
            o = _attention(qkv, qkv_c, d)
            h = _outproj(o, w_out_bf, h, g1, nw_post, name="attn_out_lat")
            if ctx_next:
                oc = _attention(None, qkv_c, d)
                hc = _outproj(oc, w_out_bf, hc, cg1, nw_post, name="attn_out_ctx")
        elif kind == 2:
            w = ssm_in[j]
            d_inner = ssm_out.shape[1]
            n_conv = ssm_conv_w.shape[2]
            p = dict(wz=w[:, :d_inner].astype(BF16), wx=w[:, d_inner:d_inner + n_conv].astype(BF16),
                     wdt=w[:, d_inner + n_conv:].astype(BF16), cw=ssm_conv_w[j], cb=ssm_conv_b[j].reshape(1, -1),
                     dt_bias=ssm_dt_bias[j], a_log=ssm_a_log[j], d_skip=ssm_d[j], norm_w=ssm_norm[j],
                     w_out=ssm_out[j].astype(BF16))
            if ctx_next:
                raise NotImplementedError("context output of the SSD mixer is not needed at this depth")
            h = _ssd_mixer(h, hc, (sc1, sh1), (csc1, csh1), nw_pre, p, g1, nw_post)
        else:
            pad_r = LANES - hy_f_w1.shape[1]
            pad_c = LANES - hy_f_w1.shape[2]
            p = dict(w_in=hy_in[j].astype(BF16), cw=hy_conv_w[j], cb=hy_conv_b[j].reshape(1, -1),
                     fw1=jnp.pad(hy_f_w1[j], ((0, pad_r), (0, pad_c))),
                     fb1=jnp.pad(hy_f_b1[j], (0, pad_c)).reshape(1, LANES),
                     fw2=jnp.pad(hy_f_w2[j], ((0, pad_c), (0, pad_c))),
                     fb2=jnp.pad(hy_f_b2[j], (0, pad_c)).reshape(1, LANES),
                     ffreq=jnp.pad(hy_f_freq[j], (0, pad_c)).reshape(1, LANES),
                     fw3=jnp.pad(hy_f_w3[j], ((0, pad_c), (0, 0))), decay=hy_decay[j].reshape(1, -1),
                     fbias=hy_f_bias[j], w_out=hy_out[j].astype(BF16))
            if ctx_next:
                raise NotImplementedError("context output of the Hyena mixer is not needed at this depth")
            h = _hyena_mixer(h, (sc1, sh1), nw_pre, p, g1, nw_post)
        w_up_bf = ffn_up[i].astype(BF16)
        w_down_bf = ffn_down[i].astype(BF16)
        h = _conv_ffn(h, sc2, sh2, g2, norm_pre_ffn[i], norm_post_ffn[i], w_up_bf, ffn_conv_w[i], ffn_conv_b[i],
                      w_down_bf, "lat")
        if ctx_next:
            hc = _conv_ffn(hc, csc2, csh2, cg2, norm_pre_ffn[i], norm_post_ffn[i], w_up_bf, ffn_conv_w[i],
                           ffn_conv_b[i], w_down_bf, "ctx")
    return h
```

```python
import functools
import math

import jax
import jax.numpy as jnp
from jax import lax
from jax.experimental import pallas as pl
from jax.experimental.pallas import tpu as pltpu

F32 = jnp.float32
BF16 = jnp.bfloat16

GRID_W = 64
N_MIXERS = 4
CTX_READER_KINDS = (1, 2)
RMS_EPS = 1e-6
FN_GROUPS = 4
HEAD_DIM = 128
N_KV_HEADS = 4
ROPE_THETA = 10000.0
ROPE_FREQS = HEAD_DIM // 4
ATTN_SCALE = HEAD_DIM ** -0.5
SSM_HEAD_DIM = 64
SSM_GROUPS = 8
SSM_STATE = 128
SSM_CHUNK = 128
SSD_GROUPS_PER_STEP = 4
HY_ORDER = 2
HY_EMB = 33
HY_BANDS = (HY_EMB - 1) // 2

V7X_VMEM_LIMIT_BYTES = 56 * 1024 * 1024
LANES = 128
CONV_HALO = 16
PROJ_ROW_CHUNK = 128
PROJ_SUB_COLS = 512
NEG_BIG = -1e30


def _cparams(*sem):
    return pltpu.CompilerParams(dimension_semantics=sem, vmem_limit_bytes=V7X_VMEM_LIMIT_BYTES)


def _tile(n, pref, mult=LANES):
    if n <= pref:
        return n
    t = (pref // mult) * mult
    while t >= mult:
        if n % t == 0:
            return t
        t -= mult
    return n


def _silu(x):
    return x * (1.0 / (1.0 + jnp.exp(-x)))


def _split_bf16(v):
    hi = v.astype(BF16)
    lo = (v - hi.astype(F32)).astype(BF16)
    return hi, lo


def _dot(a, b):
    return jnp.dot(a, b, preferred_element_type=F32)


def _dot_nt(a, b):
    return lax.dot_general(a, b, (((1,), (1,)), ((), ())), preferred_element_type=F32)


def _dot3(a, b):
    ah, al = _split_bf16(a)
    bh, bl = _split_bf16(b)
    return _dot(ah, bh) + _dot(ah, bl) + _dot(al, bh)


def _rms_rows(x, w):
    return x * lax.rsqrt(jnp.mean(x * x, axis=-1, keepdims=True) + RMS_EPS) * w


def _mod_kernel(c_ref, w_ref, b_ref, o_ref):
    a = _silu(c_ref[...]).astype(BF16)
    o_ref[0] = _dot(a, w_ref[0].astype(BF16)) + b_ref[0]


def _modulation_all(c_rows, ada_w, ada_b):
    depth, d, n6 = ada_w.shape
    r = c_rows.shape[0]
    tn = _tile(n6, 1024)
    return pl.pallas_call(
        _mod_kernel,
        grid=(depth, n6 // tn),
        in_specs=[
            pl.BlockSpec((r, d), lambda l, j: (0, 0)),
            pl.BlockSpec((1, d, tn), lambda l, j: (l, 0, j)),
            pl.BlockSpec((1, 1, tn), lambda l, j: (l, 0, j)),
        ],
        out_specs=pl.BlockSpec((1, r, tn), lambda l, j: (l, 0, j)),
        out_shape=jax.ShapeDtypeStruct((depth, r, n6), F32),
        compiler_params=_cparams("parallel", "parallel"),
        name="ada_mod",
    )(c_rows, ada_w, ada_b.reshape(depth, 1, n6))


def _proj_kernel(*refs, n_w, use_conv, n_aux, n_out, epi, tm, n_row_tiles, parity):
    refs = list(refs)
    h_ref = refs.pop(0)
    if use_conv:
        hp_ref = refs.pop(0)
        hn_ref = refs.pop(0)
    nw_ref, sc_ref, sh_ref = refs.pop(0), refs.pop(0), refs.pop(0)
    w_refs = [refs.pop(0) for _ in range(n_w)]
    cw_refs, cb_refs = [], []
    if use_conv:
        for _ in range(n_w):
            cw_refs.append(refs.pop(0))
            cb_refs.append(refs.pop(0))
    aux_refs = [refs.pop(0) for _ in range(n_aux)]
    out_refs = [refs.pop(0) for _ in range(n_out)]
    u_ref = refs.pop(0)
    p_ref = refs.pop(0) if parity else None
    i = pl.program_id(1)
    j = pl.program_id(2)
    hl = CONV_HALO
    off = hl if use_conv else 0
    rc = _tile(tm, PROJ_ROW_CHUNK, 16)

    @pl.when(j == 0)
    def _():
        wv = nw_ref[...] * (1.0 + sc_ref[0])
        shv = sh_ref[0]

        def nm(x):
            return x * lax.rsqrt(jnp.mean(x * x, axis=-1, keepdims=True) + RMS_EPS) * wv + shv

        x = h_ref[0]
        inv = lax.rsqrt(jnp.mean(x * x, axis=-1, keepdims=True) + RMS_EPS)
        for r0 in range(0, tm, rc):
            u_ref[off + r0:off + r0 + rc, :] = (h_ref[0, r0:r0 + rc, :] * inv[r0:r0 + rc] * wv + shv).astype(BF16)
        if use_conv:
            u_ref[0:hl, :] = jnp.where(i == 0, 0.0, nm(hp_ref[0])).astype(BF16)
            u_ref[hl + tm:, :] = jnp.where(i == n_row_tiles - 1, 0.0, nm(hn_ref[0])).astype(BF16)

    tn = w_refs[0].shape[1]
    if parity:
        hm = tm // 2
        ys = {0: [], 1: []}
        for k in range(n_w):
            p = _dot(u_ref[...], w_refs[k][...])
            for sl in range(tn // LANES):
                p_ref[k, sl] = p[:, sl * LANES:(sl + 1) * LANES]
        for par in (0, 1):
            for k in range(n_w):
                slabs = []
                for sl in range(tn // LANES):
                    lanes = slice(sl * LANES, (sl + 1) * LANES)
                    rows = lambda shift: p_ref[k, sl, pl.ds(off + par + shift, hm, stride=2), :]
                    if use_conv:
                        cw = cw_refs[k][:, lanes]
                        slabs.append(cw[0:1] * rows(-1) + cw[1:2] * rows(0) + cw[2:3] * rows(1)
                                     + cb_refs[k][:, lanes])
                    else:
                        slabs.append(rows(0))
                ys[par].append(jnp.concatenate(slabs, axis=1))
            epi(j, ys[par], aux_refs, [o.at[par] for o in out_refs], slice(0, tn))
        return
    if not use_conv:
        epi(j, [_dot(u_ref[...], w_refs[k][...]) for k in range(n_w)], aux_refs, out_refs, slice(0, tn))
        return
    rows_all = tm + 2 * hl
    for c0 in range(0, tn, PROJ_SUB_COLS):
        cols = slice(c0, min(c0 + PROJ_SUB_COLS, tn))
        ys = []
        for k in range(n_w):
            p = _dot(u_ref[...], w_refs[k][:, cols])
            cw = cw_refs[k][:, cols]
            prev = pltpu.roll(p, 1, axis=0)
            nxt = pltpu.roll(p, rows_all - 1, axis=0)
            y = cw[0:1] * prev + cw[1:2] * p + cw[2:3] * nxt + cb_refs[k][:, cols]
            ys.append(y[hl:hl + tm])
        epi(j, ys, aux_refs, out_refs, cols)


def _proj(h, nw, sc, sh, w_views, *, n_col_tiles, tn, tm, epi, out_shapes, out_specs,
          conv=None, aux=(), aux_specs=(), parity=False, name):
    b, t, d = h.shape
    tm = _tile(t, tm, 16)
    n_row_tiles = t // tm
    use_conv = conv is not None
    hl = CONV_HALO
    ins = [h]
    specs = [pl.BlockSpec((1, tm, d), lambda bi, i, j: (bi, i, 0))]
    if use_conv:
        r = tm // hl
        last = t // hl - 1
        ins += [h, h]
        specs += [
            pl.BlockSpec((1, hl, d), lambda bi, i, j: (bi, jnp.maximum(i * r - 1, 0), 0)),
            pl.BlockSpec((1, hl, d), lambda bi, i, j: (bi, jnp.minimum((i + 1) * r, last), 0)),
        ]
    ins += [nw.reshape(1, d), sc, sh]
    specs += [
        pl.BlockSpec((1, d), lambda bi, i, j: (0, 0)),
        pl.BlockSpec((1, 1, d), lambda bi, i, j: (bi, 0, 0)),
        pl.BlockSpec((1, 1, d), lambda bi, i, j: (bi, 0, 0)),
    ]
    for w, off in w_views:
        ins.append(w)
        specs.append(pl.BlockSpec((d, tn), lambda bi, i, j, off=off: (0, off + j)))
    if use_conv:
        cw, cb = conv
        for _, off in w_views:
            ins += [cw, cb]
            specs += [
                pl.BlockSpec((cw.shape[0], tn), lambda bi, i, j, off=off: (0, off + j)),
                pl.BlockSpec((1, tn), lambda bi, i, j, off=off: (0, off + j)),
            ]
    ins += list(aux)
    specs += list(aux_specs)
    rows = tm + 2 * hl if use_conv else tm
    scratch = [pltpu.VMEM((rows, d), BF16)]
    if parity:
        scratch.append(pltpu.VMEM((len(w_views), tn // LANES, rows, LANES), F32))
    kern = functools.partial(_proj_kernel, n_w=len(w_views), use_conv=use_conv, n_aux=len(aux),
                             n_out=len(out_shapes), epi=epi, tm=tm, n_row_tiles=n_row_tiles, parity=parity)
    return pl.pallas_call(
        kern,
        grid=(b, n_row_tiles, n_col_tiles),
        in_specs=specs,
        out_specs=out_specs(tm),
        out_shape=out_shapes,
        scratch_shapes=scratch,
        compiler_params=_cparams("parallel", "parallel", "arbitrary"),
        name=name,
    )(*ins)


def _std_out(b, t, n, tn, dtype):
    shapes = [jax.ShapeDtypeStruct((b, t, n), dtype)]
    specs = lambda tm: [pl.BlockSpec((1, tm, tn), lambda bi, i, j: (bi, i, j))]
    return shapes, specs


def _parity_out(b, t, n, tn, dtypes):
    shapes = [jax.ShapeDtypeStruct((2, b, t // 2, n), dt) for dt in dtypes]
    specs = lambda tm: [pl.BlockSpec((2, 1, tm // 2, tn), lambda bi, i, j: (0, bi, i, j)) for _ in dtypes]
    return shapes, specs


def _epi_identity(j, ys, aux_refs, out_refs, cols):
    for o_ref in out_refs:
        o_ref[0, :, cols] = ys[0].astype(o_ref.dtype)


def _epi_silu(j, ys, aux_refs, out_refs, cols):
    out_refs[0][0, :, cols] = _silu(ys[0]).astype(out_refs[0].dtype)


def _epi_swiglu(j, ys, aux_refs, out_refs, cols):
    out_refs[0][0, :, cols] = (_silu(ys[0]) * ys[1]).astype(out_refs[0].dtype)


def _outproj_kernel(a_ref, w_ref, h_ref, g_ref, nw_ref, o_ref, acc_ref, *, nj, tn):
    j = pl.program_id(2)
    acc_ref[j] = _dot(a_ref[0], w_ref[...])

    @pl.when(j == nj - 1)
    def _():
        ss = None
        for jj in range(nj):
            o = acc_ref[jj]
            s = jnp.sum(o * o, axis=-1, keepdims=True)
            ss = s if ss is None else ss + s
        inv = lax.rsqrt(ss * (1.0 / (nj * tn)) + RMS_EPS)
        for jj in range(nj):
            cols = slice(jj * tn, (jj + 1) * tn)
            o_ref[0, :, cols] = h_ref[0, :, cols] + g_ref[0, :, cols] * (acc_ref[jj] * inv * nw_ref[:, cols])


def _outproj_parity_kernel(a_ref, w_ref, h_ref, g_ref, nw_ref, o_ref, acc_ref, *, nj, tn, hm):
    j = pl.program_id(2)
    spt = tn // LANES
    for par in (0, 1):
        r = _dot(a_ref[par, 0], w_ref[...])
        for sl in range(spt):
            acc_ref[j * spt + sl, pl.ds(par, hm, stride=2), :] = r[:, sl * LANES:(sl + 1) * LANES]

    @pl.when(j == nj - 1)
    def _():
        ss = None
        for sl in range(nj * spt):
            o = acc_ref[sl]
            s = jnp.sum(o * o, axis=-1, keepdims=True)
            ss = s if ss is None else ss + s
        inv = lax.rsqrt(ss * (1.0 / (nj * tn)) + RMS_EPS)
        for sl in range(nj * spt):
            cols = slice(sl * LANES, (sl + 1) * LANES)
            o_ref[0, :, cols] = h_ref[0, :, cols] + g_ref[0, :, cols] * (acc_ref[sl] * inv * nw_ref[:, cols])


def _outproj(a, w, h, g, nw, *, tm=512, tn=512, parity=False, name):
    b, t, d = h.shape
    kdim = w.shape[0]
    tm = _tile(t, tm, 16)
    tn = _tile(d, tn)
    nj = d // tn
    if parity:
        hm = tm // 2
        return pl.pallas_call(
            functools.partial(_outproj_parity_kernel, nj=nj, tn=tn, hm=hm),
            grid=(b, t // tm, nj),
            in_specs=[
                pl.BlockSpec((2, 1, hm, kdim), lambda bi, i, j: (0, bi, i, 0)),
                pl.BlockSpec((kdim, tn), lambda bi, i, j: (0, j)),
                pl.BlockSpec((1, tm, d), lambda bi, i, j: (bi, i, 0)),
                pl.BlockSpec((1, 1, d), lambda bi, i, j: (bi, 0, 0)),
                pl.BlockSpec((1, d), lambda bi, i, j: (0, 0)),
            ],
            out_specs=pl.BlockSpec((1, tm, d), lambda bi, i, j: (bi, i, 0)),
            out_shape=jax.ShapeDtypeStruct((b, t, d), F32),
            scratch_shapes=[pltpu.VMEM((d // LANES, tm, LANES), F32)],
            compiler_params=_cparams("parallel", "parallel", "arbitrary"),
            name=name,
        )(a, w, h, g, nw.reshape(1, d))
    return pl.pallas_call(
        functools.partial(_outproj_kernel, nj=nj, tn=tn),
        grid=(b, t // tm, nj),
        in_specs=[
            pl.BlockSpec((1, tm, kdim), lambda bi, i, j: (bi, i, 0)),
            pl.BlockSpec((kdim, tn), lambda bi, i, j: (0, j)),
            pl.BlockSpec((1, tm, d), lambda bi, i, j: (bi, i, 0)),
            pl.BlockSpec((1, 1, d), lambda bi, i, j: (bi, 0, 0)),
            pl.BlockSpec((1, d), lambda bi, i, j: (0, 0)),
        ],
        out_specs=pl.BlockSpec((1, tm, d), lambda bi, i, j: (bi, i, 0)),
        out_shape=jax.ShapeDtypeStruct((b, t, d), F32),
        scratch_shapes=[pltpu.VMEM((nj, tm, tn), F32)],
        compiler_params=_cparams("parallel", "parallel", "arbitrary"),
        name=name,
    )(a, w, h, g, nw.reshape(1, d))


def _dft_kernel(*refs, n_l, n_r, n_acc, terms, n_aux, n_out, epi):
    refs = list(refs)
    l_refs = [refs.pop(0) for _ in range(n_l)]
    r_refs = [refs.pop(0) for _ in range(n_r)]
    aux_refs = [refs.pop(0) for _ in range(n_aux)]
    out_refs = [refs.pop(0) for _ in range(n_out)]
    accs = [None] * n_acc
    for o, l, r in terms:
        dd = _dot(l_refs[l][0], r_refs[r][0])
        accs[o] = dd if accs[o] is None else accs[o] + dd
    epi(pl.program_id(1), accs, aux_refs, out_refs)


def _dft_call(l_tabs, r_views, terms, n_acc, *, n_par, batch, m, kdim, n, tm, tn, epi, out_shapes, out_specs,
              aux=(), aux_specs=None, name):
    tm = _tile(m, tm, 16)
    tn = _tile(n, tn)
    ins, specs = [], []
    for lt in l_tabs:
        ins.append(lt)
        specs.append(pl.BlockSpec((1, tm, kdim), lambda p, i, j, b: (p, i, 0)))
    for arr, lead_fn, col_fn in r_views:
        ins.append(arr)
        specs.append(pl.BlockSpec((1, kdim, tn),
                                  lambda p, i, j, b, lf=lead_fn, cf=col_fn: (lf(p, b), 0, cf(p, j))))
    ins += list(aux)
    if aux_specs is not None:
        specs += list(aux_specs(tm, tn))
    kern = functools.partial(_dft_kernel, n_l=len(l_tabs), n_r=len(r_views), n_acc=n_acc, terms=terms,
                             n_aux=len(aux), n_out=len(out_shapes), epi=epi)
    return pl.pallas_call(
        kern,
        grid=(n_par, m // tm, n // tn, batch),
        in_specs=specs,
        out_specs=out_specs(tm, tn),
        out_shape=out_shapes,
        compiler_params=_cparams("parallel", "parallel", "parallel", "parallel"),
        name=name,
    )(*ins)


def _int_grid(n):
    k = lax.broadcasted_iota(jnp.int32, (n, n), 0)
    j = lax.broadcasted_iota(jnp.int32, (n, n), 1)
    return k, j


def _fourier_pos_tables(n):
    k, j = _int_grid(n // 2)
    tabs = []
    for par in (0, 1):
        ang = ((k * (2 * j + par)) % n).astype(F32) * (2.0 * math.pi / n)
        tabs += [jnp.cos(ang).astype(BF16)[None], (-jnp.sin(ang)).astype(BF16)[None]]
    return tabs


def _fourier_chan_tables(gdim):
    kc, jc = _int_grid(gdim)
    angc = ((kc * jc) % gdim).astype(F32) * (2.0 * math.pi / gdim)
    return jnp.cos(angc), jnp.sin(angc)


def _wfold_kernel(w_ref, c_ref, s_ref, oc_ref, os_ref):
    w = w_ref[...]
    oc_ref[...] = _dot3(w, c_ref[...]).astype(BF16)
    os_ref[...] = _dot3(w, s_ref[...]).astype(BF16)


def _fold_channel_dft(w_in, cc, sc):
    d = w_in.shape[0]
    gdim = cc.shape[0]
    ng = d // gdim
    oc, os_ = pl.pallas_call(
        _wfold_kernel,
        grid=(ng,),
        in_specs=[
            pl.BlockSpec((d, gdim), lambda g: (0, g)),
            pl.BlockSpec((gdim, gdim), lambda g: (0, 0)),
            pl.BlockSpec((gdim, gdim), lambda g: (0, 0)),
        ],
        out_specs=[pl.BlockSpec((d, gdim), lambda g: (0, g)), pl.BlockSpec((d, gdim), lambda g: (0, g))],
        out_shape=[jax.ShapeDtypeStruct((d, d), BF16), jax.ShapeDtypeStruct((d, d), BF16)],
        compiler_params=_cparams("parallel"),
        name="fourier_fold",
    )(w_in, cc, sc)
    return jnp.concatenate([oc, os_], axis=1)


def _fourier_mixer(h, nw, sc, sh, w_fold, tables, w_out_bf, g, nw_post, tag):
    b, t, d = h.shape
    half = t // 2
    tn = _tile(2 * d, 512)
    shapes, specs = _parity_out(b, t, 2 * d, tn, [BF16])
    a = _proj(h, nw, sc, sh, [(w_fold, 0)], n_col_tiles=2 * d // tn, tn=tn, tm=1024, epi=_epi_identity,
              out_shapes=shapes, out_specs=specs, parity=True, name="fourier_in_" + tag)[0]
    scale = 1.0 / math.sqrt(t * (d // FN_GROUPS))

    def epi(i, accs, aux_refs, out_refs):
        ev, od = accs
        out_refs[0][0, 0] = ((ev + od) * scale).astype(BF16)
        out_refs[0][0, 1] = ((ev - od) * scale).astype(BF16)

    a2 = a.reshape(2 * b, half, 2 * d)
    tn2 = _tile(d, 512)
    nj = d // tn2
    views = [(a2, lambda p, bi, par=par: par * b + bi, lambda p, j, q=q: q * nj + j)
             for par in (0, 1) for q in (0, 1)]
    f = _dft_call(
        tables, views, [(0, 0, 0), (0, 1, 1), (1, 2, 2), (1, 3, 3)], 2,
        n_par=1, batch=b, m=half, kdim=half, n=d, tm=512, tn=tn2, epi=epi,
        out_shapes=[jax.ShapeDtypeStruct((b, 2, half, d), BF16)],
        out_specs=lambda tm, tn_: [pl.BlockSpec((1, 2, tm, tn_), lambda p, i, j, bi: (bi, 0, i, j))],
        name="fourier_pos_" + tag)[0]
    return _outproj(f.reshape(b, t, d), w_out_bf, h, g, nw_post, name="fourier_out_" + tag)


def _rope_tables(n_tokens):
    rows = n_tokens // GRID_W
    row = jnp.broadcast_to(jnp.arange(rows, dtype=F32)[:, None], (rows, GRID_W)).reshape(n_tokens)
    col = jnp.broadcast_to(jnp.arange(GRID_W, dtype=F32)[None, :], (rows, GRID_W)).reshape(n_tokens)
    inv_freq = ROPE_THETA ** (-jnp.arange(ROPE_FREQS, dtype=F32) / ROPE_FREQS)
    ar, ac = row[:, None] * inv_freq, col[:, None] * inv_freq
    cos = jnp.concatenate([jnp.cos(ar), jnp.cos(ar), jnp.cos(ac), jnp.cos(ac)], axis=1)
    sin = jnp.concatenate([-jnp.sin(ar), jnp.sin(ar), -jnp.sin(ac), jnp.sin(ac)], axis=1)
    return cos, sin


def _attn_in_epi(j, ys, aux_refs, out_refs, cols, *, n_q_tiles, heads_per_tile, rope):
    y = ys[0]
    gain_ref = aux_refs[0]
    o_ref = out_refs[0]

    @pl.when(j <= n_q_tiles)
    def _():
        gain = gain_ref[...] * jnp.where(j < n_q_tiles, ATTN_SCALE, 1.0)
        outs = []
        for hh in range(heads_per_tile):
            t = y[:, hh * HEAD_DIM:(hh + 1) * HEAD_DIM]
            t = t * lax.rsqrt(jnp.mean(t * t, axis=-1, keepdims=True) + RMS_EPS)
            t = t * gain[:, hh * HEAD_DIM:(hh + 1) * HEAD_DIM]
            if rope:
                cos, sin = aux_refs[1][...], aux_refs[2][...]
                lane = lax.broadcasted_iota(jnp.int32, t.shape, 1)
                half = ROPE_FREQS
                partner = jnp.where(lane % (2 * half) < half,
                                    pltpu.roll(t, HEAD_DIM - half, axis=1), pltpu.roll(t, half, axis=1))
                t = t * cos + partner * sin
            outs.append(t)
        o_ref[0] = jnp.concatenate(outs, axis=1).astype(o_ref.dtype)

    @pl.when(j > n_q_tiles)
    def _():
        o_ref[0] = y.astype(o_ref.dtype)


def _attn_project(h, nw, sc, sh, w_in_bf, gains, rope_tabs, tag):
    b, t, d = h.shape
    n = w_in_bf.shape[1]
    tn = N_KV_HEADS * HEAD_DIM
    n_q_tiles = d // tn
    rope = rope_tabs is not None
    aux = [gains]
    aux_specs = [pl.BlockSpec((1, tn), lambda bi, i, j: (0, jnp.minimum(j, n_q_tiles)))]
    tm = _tile(t, 1024, 16)
    if rope:
        aux += list(rope_tabs)
        aux_specs += [pl.BlockSpec((tm, HEAD_DIM), lambda bi, i, j: (i, 0))] * 2
    epi = functools.partial(_attn_in_epi, n_q_tiles=n_q_tiles, heads_per_tile=tn // HEAD_DIM, rope=rope)
    shapes, specs = _std_out(b, t, n, tn, BF16)
    return _proj(h, nw, sc, sh, [(w_in_bf, 0)], n_col_tiles=n // tn, tn=tn, tm=tm, epi=epi,
                 out_shapes=shapes, out_specs=specs, aux=aux, aux_specs=aux_specs, name="attn_in_" + tag)[0]


def _attn_kernel(*refs, has_lat, tkv, n_lat_chunks):
    if has_lat:
        q_ref, kc_ref, vc_ref, k_ref, v_ref, o_ref = refs
    else:
        q_ref, kc_ref, vc_ref, o_ref = refs
    q = q_ref[0]
    s = _dot_nt(q, kc_ref[0])
    m = jnp.max(s, axis=-1, keepdims=True)
    p = jnp.exp(s - m)
    l = jnp.sum(p, axis=-1, keepdims=True)
    acc = _dot(p.astype(BF16), vc_ref[0])
    if has_lat:
        for c in range(n_lat_chunks):
            kk = k_ref[0, c * tkv:(c + 1) * tkv, :]
            vv = v_ref[0, c * tkv:(c + 1) * tkv, :]
            s = _dot_nt(q, kk)
            m_new = jnp.maximum(m, jnp.max(s, axis=-1, keepdims=True))
            alpha = jnp.exp(m - m_new)
            p = jnp.exp(s - m_new)
            l = alpha * l + jnp.sum(p, axis=-1, keepdims=True)
            acc = alpha * acc + _dot(p.astype(BF16), vv)
            m = m_new
    o_ref[0] = (acc / l).astype(o_ref.dtype)


def _attention(qkv, qkv_ctx, d, *, tq=512, tkv=1024):
    has_lat = qkv is not None
    src = qkv if has_lat else qkv_ctx
    b, t, _ = src.shape
    tc = qkv_ctx.shape[1]
    n_heads = d // HEAD_DIM
    grp = n_heads // N_KV_HEADS
    k_off = n_heads
    v_off = n_heads + N_KV_HEADS
    tq = _tile(t, tq, 16)
    tkv = _tile(t, tkv)
    ins = [src, qkv_ctx, qkv_ctx]
    specs = [
        pl.BlockSpec((1, tq, HEAD_DIM), lambda bi, hh, i: (bi, i, hh)),
        pl.BlockSpec((1, tc, HEAD_DIM), lambda bi, hh, i: (bi, 0, k_off + hh // grp)),
        pl.BlockSpec((1, tc, HEAD_DIM), lambda bi, hh, i: (bi, 0, v_off + hh // grp)),
    ]
    if has_lat:
        ins += [qkv, qkv]
        specs += [
            pl.BlockSpec((1, t, HEAD_DIM), lambda bi, hh, i: (bi, 0, k_off + hh // grp)),
            pl.BlockSpec((1, t, HEAD_DIM), lambda bi, hh, i: (bi, 0, v_off + hh // grp)),
        ]
    return pl.pallas_call(
        functools.partial(_attn_kernel, has_lat=has_lat, tkv=tkv, n_lat_chunks=t // tkv),
        grid=(b, n_heads, t // tq),
        in_specs=specs,
        out_specs=pl.BlockSpec((1, tq, HEAD_DIM), lambda bi, hh, i: (bi, i, hh)),
        out_shape=jax.ShapeDtypeStruct((b, t, d), BF16),
        compiler_params=_cparams("parallel", "parallel", "parallel"),
        name="attn_core_lat" if has_lat else "attn_core_ctx",
    )(*ins)


def _dt_epi(j, ys, aux_refs, out_refs, cols):
    x = ys[0] + aux_refs[0][...]
    dt = jnp.maximum(x, 0.0) + jnp.log(1.0 + jnp.exp(-jnp.abs(x)))
    out_refs[0][0] = dt
    out_refs[1][0] = dt * (-jnp.exp(aux_refs[1][...]))


def _ssd_kernel(*refs, rev, with_output, final, nc):
    refs = list(refs)
    x_ref, b_ref, c_ref, dt_ref, adt_ref, s0_ref = [refs.pop(0) for _ in range(6)]
    if final:
        yprev_ref, z_ref, dskip_ref, nw_ref = [refs.pop(0) for _ in range(4)]
    y_ref = refs.pop(0) if with_output else None
    sfin_ref = refs.pop(0)
    st_ref = refs.pop(0)
    c = pl.program_id(2)
    q = SSM_CHUNK
    hp = SSM_HEAD_DIM
    ns = SSM_STATE
    gpb = dt_ref.shape[1]
    n_hg = dt_ref.shape[2]
    width = n_hg * hp

    @pl.when(c == 0)
    def _():
        st_ref[...] = s0_ref[0]

    ii = lax.broadcasted_iota(jnp.int32, (q, q), 0)
    jj = lax.broadcasted_iota(jnp.int32, (q, q), 1)
    mask = (jj >= ii) if rev else (jj <= ii)
    mask_bf = mask.astype(F32).astype(BF16)
    mask_t_bf = ((ii >= jj) if rev else (ii <= jj)).astype(F32).astype(BF16)
    eye_bf = (ii == jj).astype(F32).astype(BF16)
    lane = lax.broadcasted_iota(jnp.int32, (q, 2 * hp), 1)

    def hi_lo(v):
        hi = v.astype(BF16).astype(F32)
        return hi, v - hi

    def expand_rows(v):
        return jnp.concatenate([jnp.broadcast_to(v[hh:hh + 1, :], (hp, q)) for hh in range(n_hg)], axis=0)

    groups = range(gpb)
    cols = [slice(gi * width, (gi + 1) * width) for gi in groups]
    scols = [slice(gi * ns, (gi + 1) * ns) for gi in groups]
    dt_r = [dt_ref[0, gi] for gi in groups]
    a_parts = [hi_lo(adt_ref[0, gi]) for gi in groups]
    d_parts = [hi_lo(dt_r[gi]) for gi in groups]
    e_acs = [_dot_nt(mask_bf, expand_rows(a_parts[gi][0]).astype(BF16))
             + _dot_nt(mask_bf, expand_rows(a_parts[gi][1]).astype(BF16)) for gi in groups]
    e_dt = [_dot_nt(eye_bf, expand_rows(d_parts[gi][0]).astype(BF16))
            + _dot_nt(eye_bf, expand_rows(d_parts[gi][1]).astype(BF16)) for gi in groups]
    last = [e_acs[gi][0:1, :] if rev else e_acs[gi][q - 1:q, :] for gi in groups]

    x = [x_ref[0, :, cols[gi]] for gi in groups]
    bm = [b_ref[0, :, scols[gi]] for gi in groups]
    s_old = [st_ref[gi] for gi in groups]
    for gi in groups:
        wend = jnp.exp(last[gi] - e_acs[gi]) * e_dt[gi]
        s_new = s_old[gi] * jnp.exp(last[gi]) + _dot(bm[gi].T.astype(BF16), (x[gi] * wend).astype(BF16))
        st_ref[gi] = s_new

        @pl.when(c == nc - 1)
        def _():
            sfin_ref[0, gi] = s_new

    if not with_output:
        return
    cm_bf = [c_ref[0, :, scols[gi]].astype(BF16) for gi in groups]
    cb = [_dot_nt(cm_bf[gi], bm[gi].astype(BF16)) for gi in groups]
    acs_r = []
    for gi in groups:
        r16 = _dot(jnp.concatenate(a_parts[gi], axis=0).astype(BF16), mask_t_bf)
        acs_r.append(r16[0:n_hg] + r16[n_hg:2 * n_hg])
    y_state = [_dot(cm_bf[gi], s_old[gi].astype(BF16)) * jnp.exp(e_acs[gi]) for gi in groups]
    ys = [[] for _ in groups]
    for pair in range(n_hg // 2):
        for gi in groups:
            ws = []
            for hh in (2 * pair, 2 * pair + 1):
                seg = e_acs[gi][:, hh * hp:hh * hp + 1] - acs_r[gi][hh:hh + 1, :]
                decay = jnp.exp(jnp.where(mask, seg, NEG_BIG))
                ws.append((cb[gi] * decay * dt_r[gi][hh:hh + 1, :]).astype(BF16))
            xp = x[gi][:, pair * 2 * hp:(pair + 1) * 2 * hp]
            bd = jnp.concatenate([jnp.where(lane < hp, xp, 0.0), jnp.where(lane >= hp, xp, 0.0)], axis=0)
            ys[gi].append(_dot(jnp.concatenate(ws, axis=1), bd.astype(BF16)))
    for gi in groups:
        y = jnp.concatenate(ys[gi], axis=1) + y_state[gi]
        if final:
            yt = yprev_ref[0, :, cols[gi]] + y + x[gi] * dskip_ref[:, cols[gi]]
            yt = yt * _silu(z_ref[0, :, cols[gi]])
            y_ref[0, :, cols[gi]] = _rms_rows(yt, nw_ref[:, cols[gi]]).astype(y_ref.dtype)
        else:
            y_ref[0, :, cols[gi]] = y


def _ssd_scan(xbc, dt_rows, adt_rows, s0, d_inner, *, rev, with_output, final_args=None):
    b, t, _ = xbc.shape
    g = SSM_GROUPS
    q = SSM_CHUNK
    nc = t // q
    width = d_inner // g
    n_hg = width // SSM_HEAD_DIM
    gpb = SSD_GROUPS_PER_STEP
    ngb = g // gpb
    wblk = gpb * width
    sblk = gpb * SSM_STATE
    b_off = d_inner // sblk
    c_off = b_off + ngb
    final = final_args is not None
    cmap = (lambda c: nc - 1 - c) if rev else (lambda c: c)
    ins = [xbc, xbc, xbc, dt_rows, adt_rows, s0]
    specs = [
        pl.BlockSpec((1, q, wblk), lambda bi, gi, c: (bi, cmap(c), gi)),
        pl.BlockSpec((1, q, sblk), lambda bi, gi, c: (bi, cmap(c), b_off + gi)),
        pl.BlockSpec((1, q, sblk), lambda bi, gi, c: (bi, cmap(c), c_off + gi)),
        pl.BlockSpec((1, gpb, n_hg, q), lambda bi, gi, c: (bi, gi, 0, cmap(c))),
        pl.BlockSpec((1, gpb, n_hg, q), lambda bi, gi, c: (bi, gi, 0, cmap(c))),
        pl.BlockSpec((1, gpb, SSM_STATE, width), lambda bi, gi, c: (bi, gi, 0, 0)),
    ]
    if final:
        yprev, z, dskip, nw = final_args
        ins += [yprev, z, dskip, nw]
        specs += [
            pl.BlockSpec((1, q, wblk), lambda bi, gi, c: (bi, cmap(c), gi)),
            pl.BlockSpec((1, q, wblk), lambda bi, gi, c: (bi, cmap(c), gi)),
            pl.BlockSpec((1, wblk), lambda bi, gi, c: (0, gi)),
            pl.BlockSpec((1, wblk), lambda bi, gi, c: (0, gi)),
        ]
    out_shapes, out_specs = [], []
    if with_output:
        out_shapes.append(jax.ShapeDtypeStruct((b, t, d_inner), BF16 if final else F32))
        out_specs.append(pl.BlockSpec((1, q, wblk), lambda bi, gi, c: (bi, cmap(c), gi)))
    out_shapes.append(jax.ShapeDtypeStruct(s0.shape, F32))
    out_specs.append(pl.BlockSpec((1, gpb, SSM_STATE, width), lambda bi, gi, c: (bi, gi, 0, 0)))
    return pl.pallas_call(
        functools.partial(_ssd_kernel, rev=rev, with_output=with_output, final=final, nc=nc),
        grid=(b, ngb, nc),
        in_specs=specs,
        out_specs=out_specs,
        out_shape=out_shapes,
        scratch_shapes=[pltpu.VMEM((gpb, SSM_STATE, width), F32)],
        compiler_params=_cparams("parallel", "parallel", "arbitrary"),
        name="ssd_scan_%s%s" % ("bwd" if rev else "fwd", "" if with_output else "_state"),
    )(*ins)


def _ssd_project(h, nw, sc, sh, wz, wx, wdt, cw, cb, dt_bias, a_log, need_z, tag):
    b, t, d = h.shape
    d_inner = wz.shape[1]
    z = None
    if need_z:
        tn = _tile(d_inner, 512)
        shapes, specs = _std_out(b, t, d_inner, tn, F32)
        z = _proj(h, nw, sc, sh, [(wz, 0)], n_col_tiles=d_inner // tn, tn=tn, tm=1024, epi=_epi_identity,
                  out_shapes=shapes, out_specs=specs, name="ssd_in_z_" + tag)[0]
    nx = wx.shape[1]
    tn = _tile(nx, 512)
    shapes, specs = _std_out(b, t, nx, tn, F32)
    xbc = _proj(h, nw, sc, sh, [(wx, 0)], n_col_tiles=nx // tn, tn=tn, tm=1024, epi=_epi_silu,
                conv=(cw, cb), out_shapes=shapes, out_specs=specs, name="ssd_in_xbc_" + tag)[0]
    nd = wdt.shape[1]
    shapes = [jax.ShapeDtypeStruct((b, t, nd), F32)] * 2
    specs = lambda tm: [pl.BlockSpec((1, tm, nd), lambda bi, i, j: (bi, i, 0))] * 2
    dt, adt = _proj(h, nw, sc, sh, [(wdt, 0)], n_col_tiles=1, tn=nd, tm=1024, epi=_dt_epi,
                    out_shapes=shapes, out_specs=specs,
                    aux=[dt_bias.reshape(1, nd), a_log.reshape(1, nd)],
                    aux_specs=[pl.BlockSpec((1, nd), lambda bi, i, j: (0, 0))] * 2, name="ssd_in_dt_" + tag)

    def rows(v):
        return jnp.transpose(v.reshape(b, t, 2, SSM_GROUPS, -1), (2, 0, 3, 4, 1))

    return z, xbc, rows(dt), rows(adt)


def _ssd_mixer(h, hc, mods, mods_c, nw_pre, p, g, nw_post):
    b, t, d = h.shape
    wz, wx, wdt = p["wz"], p["wx"], p["wdt"]
    d_inner = wz.shape[1]
    sc, sh = mods
    scc, shc = mods_c
    z, xbc, dt_r, adt_r = _ssd_project(h, nw_pre, sc, sh, wz, wx, wdt, p["cw"], p["cb"], p["dt_bias"],
                                       p["a_log"], True, "lat")
    _, xbc_c, dt_rc, adt_rc = _ssd_project(hc, nw_pre, scc, shc, wz, wx, wdt, p["cw"], p["cb"], p["dt_bias"],
                                           p["a_log"], False, "ctx")
    s0 = jnp.zeros((b, SSM_GROUPS, SSM_STATE, d_inner // SSM_GROUPS), F32)
    (s_f,) = _ssd_scan(xbc_c, dt_rc[0], adt_rc[0], s0, d_inner, rev=False, with_output=False)
    (s_b,) = _ssd_scan(xbc_c, dt_rc[1], adt_rc[1], s0, d_inner, rev=True, with_output=False)
    y_f, _ = _ssd_scan(xbc, dt_r[0], adt_r[0], s_f, d_inner, rev=False, with_output=True)
    dskip = jnp.repeat(p["d_skip"], SSM_HEAD_DIM).reshape(1, d_inner)
    yn, _ = _ssd_scan(xbc, dt_r[1], adt_r[1], s_b, d_inner, rev=True, with_output=True,
                      final_args=(y_f, z, dskip, p["norm_w"].reshape(1, d_inner)))
    return _outproj(yn, p["w_out"], h, g, nw_post, name="ssd_out")


def _hyena_tables(n):
    half = n // 2
    k, s = _int_grid(half)
    def trig(m):
        ang = (m % (2 * n)).astype(F32) * (math.pi / n)
        return jnp.cos(ang), jnp.sin(ang)
    ce, se = trig(2 * s * k)
    co, so = trig((2 * s + 1) * k)
    cot, sot = trig((2 * k + 1) * s)
    alt_s = jnp.where(s % 2 == 0, 1.0, -1.0)
    alt_k = jnp.where(k % 2 == 0, 1.0, -1.0)
    sep = jnp.where(k == 0, alt_s, se)
    sop = jnp.where(k == 0, alt_s, so)
    g = jnp.where(k == 0, alt_s, -se)
    nsept = jnp.where(s == 0, -alt_k, -se)
    nsopt = jnp.where(s == 0, -alt_k, -sot)
    bf = lambda *ts: jnp.stack([t.astype(BF16) for t in ts])
    return dict(ce=bf(ce), co=bf(co), sep=bf(sep), sop=bf(sop), g=bf(g), nsop=bf(-sop),
                inv_c=bf(ce, cot), inv_s=bf(nsept, nsopt))


def _hyena_feats(n):
    t01 = jnp.linspace(0.0, 1.0, n, dtype=F32)[:, None]
    w = (2.0 * math.pi / n) * jnp.arange(n, dtype=F32)[:, None]
    f = jnp.linspace(1e-4, HY_BANDS - 1, HY_BANDS, dtype=F32)[None, :]
    feats = jnp.concatenate([t01, jnp.cos(f * w), -jnp.sin(f * w)], axis=-1)
    return jnp.pad(feats, ((0, 0), (0, LANES - HY_EMB)))


def _taps_kernel(feats_ref, w1_ref, b1_ref, w2_ref, b2_ref, fr_ref, w3_ref, dec_ref, o_ref, hid_ref, *,
                 tiles_per_dir):
    j = pl.program_id(0)

    @pl.when(j == 0)
    def _():
        fr = fr_ref[...]
        h1 = jnp.sin(fr * (_dot3(feats_ref[...], w1_ref[...]) + b1_ref[...]))
        hid_ref[...] = jnp.sin(fr * (_dot3(h1, w2_ref[...]) + b2_ref[...]))

    taps = _dot3(hid_ref[...], w3_ref[...])
    taps = taps * jnp.exp(-feats_ref[:, 0:1] * jnp.abs(dec_ref[...]))
    is_bwd = (j // tiles_per_dir) % 2 == 1
    row = lax.broadcasted_iota(jnp.int32, taps.shape, 0)
    o_ref[0] = jnp.where((row == 0) & is_bwd, 0.0, taps).astype(o_ref.dtype)


def _hyena_taps(n, p, d):
    feats = _hyena_feats(n)
    tn = _tile(d, 256)
    tpd = d // tn
    ncols = HY_ORDER * 2 * d

    def omap(j):
        return ((j // tpd) % 2, 0, (j // (2 * tpd)) * tpd + j % tpd)

    full = lambda j: (0, 0)
    return pl.pallas_call(
        functools.partial(_taps_kernel, tiles_per_dir=tpd),
        grid=(ncols // tn,),
        in_specs=[
            pl.BlockSpec((n, LANES), full),
            pl.BlockSpec((LANES, LANES), full), pl.BlockSpec((1, LANES), full),
            pl.BlockSpec((LANES, LANES), full), pl.BlockSpec((1, LANES), full),
            pl.BlockSpec((1, LANES), full),
            pl.BlockSpec((LANES, tn), lambda j: (0, j)),
            pl.BlockSpec((1, tn), lambda j: (0, j)),
        ],
        out_specs=pl.BlockSpec((1, n, tn), omap),
        out_shape=jax.ShapeDtypeStruct((2, n, HY_ORDER * d), BF16),
        scratch_shapes=[pltpu.VMEM((n, LANES), F32)],
        compiler_params=_cparams("arbitrary"),
        name="hyena_taps",
    )(feats, p["fw1"], p["fb1"], p["fw2"], p["fb2"], p["ffreq"], p["fw3"], p["decay"])


def _first_row(shape, i):
    return (lax.broadcasted_iota(jnp.int32, shape, 0) == 0) & (i == 0)


def _hyena_spectrum(taps, tabs, n, d):
    half = n // 2
    ncol = HY_ORDER * d
    taps2 = taps.reshape(2, half, 2 * ncol)
    tn = _tile(ncol, 512)
    nj = ncol // tn
    views = [(taps2, lambda p, b, dr=dr: dr, lambda p, j, par=par: par * nj + j)
             for dr in (0, 1) for par in (0, 1)]
    l_tabs = [tabs["ce"], tabs["co"], tabs["sep"], tabs["sop"], tabs["g"], tabs["nsop"]]
    terms = [(0, 0, 0), (0, 0, 2), (1, 1, 1), (1, 1, 3), (2, 2, 0), (2, 4, 2), (3, 3, 1), (3, 5, 3)]

    def epi(i, accs, aux_refs, out_refs):
        pec, poc, pes, pos = accs
        first = _first_row(pec.shape, i)
        out_refs[0][...] = pec + poc
        out_refs[1][...] = jnp.where(first, pes, -(pes + pos))
        out_refs[2][...] = pec - poc
        out_refs[3][...] = jnp.where(first, -pos, pes - pos)

    return _dft_call(
        l_tabs, views, terms, 4, n_par=1, batch=1, m=half, kdim=half, n=ncol, tm=256, tn=tn, epi=epi,
        out_shapes=[jax.ShapeDtypeStruct((half, ncol), F32)] * 4,
        out_specs=lambda tm, tn_: [pl.BlockSpec((tm, tn_), lambda p, i, j, b: (i, j))] * 4,
        name="hyena_spectrum")


def _hyena_long_conv(src_bf, src_f32, gate, gate_off, spec, order, bias, tabs, d, out_dtypes, tag):
    b2, half, _ = src_bf.shape
    b = b2 // 2
    n = 2 * half
    inv_n = 1.0 / n
    tn = _tile(d, 512)
    nj = d // tn

    def fwd_epi(i, accs, aux_refs, out_refs):
        pec, poc, pes, pos = accs
        hr, hi, hur, hui = [a[...] for a in aux_refs]
        vr, ur = pec + poc, pec - poc
        vi, ui = -(pes + pos), pes - pos
        yr, yi = vr * hr - vi * hi, vr * hi + vi * hr
        yur, yui = ur * hur - ui * hui, ur * hui + ui * hur
        first = _first_row(pec.shape, i)
        y0, yny = vr * hr, ur * hur
        mr, mi = pes, -pos
        ymr, ymi = mr * hi - mi * hui, mr * hui + mi * hi
        ae = jnp.where(first, 0.5 * (y0 + yny), yr + yur)
        be = jnp.where(first, -ymr, yi - yui)
        ao = jnp.where(first, 0.5 * (y0 - yny), yr - yur)
        bo = jnp.where(first, ymi, yi + yui)
        for q, v in enumerate((ae, be, ao, bo)):
            out_refs[0][0, :, q * tn:(q + 1) * tn] = (v * inv_n).astype(BF16)

    views = [(src_bf, lambda p, bi: bi, lambda p, j: j), (src_bf, lambda p, bi: b + bi, lambda p, j: j)]
    yab = _dft_call(
        [tabs["ce"], tabs["co"], tabs["sep"], tabs["sop"]], views,
        [(0, 0, 0), (1, 1, 1), (2, 2, 0), (3, 3, 1)], 4,
        n_par=1, batch=b, m=half, kdim=half, n=d, tm=512, tn=tn, epi=fwd_epi,
        out_shapes=[jax.ShapeDtypeStruct((b, half, 4 * d), BF16)],
        out_specs=lambda tm, tn_: [pl.BlockSpec((1, tm, 4 * tn_), lambda p, i, j, bi: (bi, i, j))],
        aux=list(spec),
        aux_specs=lambda tm, tn_: [pl.BlockSpec((tm, tn_), lambda p, i, j, bi: (i, order * nj + j))] * 4,
        name="hyena_fwd_" + tag)[0]

    def inv_epi(i, accs, aux_refs, out_refs):
        val = aux_refs[1][0] * (accs[0] + aux_refs[0][0] * aux_refs[2][...])
        for o_ref in out_refs:
            o_ref[0] = val.astype(o_ref.dtype)

    return _dft_call(
        [tabs["inv_c"], tabs["inv_s"]],
        [(yab, lambda p, bi: bi, lambda p, j: 4 * j + 2 * p), (yab, lambda p, bi: bi, lambda p, j: 4 * j + 2 * p + 1)],
        [(0, 0, 0), (0, 1, 1)], 1,
        n_par=2, batch=b, m=half, kdim=half, n=d, tm=1024, tn=tn, epi=inv_epi,
        out_shapes=[jax.ShapeDtypeStruct((2 * b, half, d), dt) for dt in out_dtypes],
        out_specs=lambda tm, tn_: [pl.BlockSpec((1, tm, tn_), lambda p, i, j, bi: (p * b + bi, i, j))
                                   for _ in out_dtypes],
        aux=[src_f32, gate, bias],
        aux_specs=lambda tm, tn_: [
            pl.BlockSpec((1, tm, tn_), lambda p, i, j, bi: (p * b + bi, i, j)),
            pl.BlockSpec((1, tm, tn_), lambda p, i, j, bi: (p * b + bi, i, gate_off * nj + j)),
            pl.BlockSpec((1, tn_), lambda p, i, j, bi: (0, j)),
        ],
        name="hyena_inv_" + tag)


def _hyena_mixer(h, mods, nw_pre, p, g, nw_post):
    b, t, d = h.shape
    sc, sh = mods
    tn = _tile(3 * d, 512)
    half = t // 2
    shapes, specs = _parity_out(b, t, 3 * d, tn, [F32, BF16])
    pj, pj_bf = _proj(h, nw_pre, sc, sh, [(p["w_in"], 0)], n_col_tiles=3 * d // tn, tn=tn, tm=1024,
                      epi=_epi_identity, conv=(p["cw"], p["cb"]), out_shapes=shapes, out_specs=specs,
                      parity=True, name="hyena_in")
    tabs = _hyena_tables(t)
    taps = _hyena_taps(t, p, d)
    spec = _hyena_spectrum(taps, tabs, t, d)
    fbias = p["fbias"]
    pj2 = pj.reshape(2 * b, half, 3 * d)
    z, z_bf = _hyena_long_conv(pj_bf.reshape(2 * b, half, 3 * d), pj2, pj2, 1, spec, 0, fbias[0:1], tabs, d,
                               (F32, BF16), "o1")
    (y,) = _hyena_long_conv(z_bf, z, pj2, 2, spec, 1, fbias[1:2], tabs, d, (BF16,), "o2")
    return _outproj(y.reshape(2, b, half, d), p["w_out"], h, g, nw_post, parity=True, name="hyena_out")


def _conv_ffn(h, sc, sh, g, nw_pre, nw_post, w_up_bf, cw, cb, w_down_bf, tag):
    b, t, d = h.shape
    dff = w_down_bf.shape[0]
    tn = _tile(dff, 512)
    nj = dff // tn
    shapes, specs = _std_out(b, t, dff, tn, BF16)
    act = _proj(h, nw_pre, sc, sh, [(w_up_bf, 0), (w_up_bf, nj)], n_col_tiles=nj, tn=tn, tm=1024,
                epi=_epi_swiglu, conv=(cw, cb.reshape(1, -1)), out_shapes=shapes, out_specs=specs,
                name="ffn_up_" + tag)[0]
    return _outproj(act, w_down_bf, h, g, nw_post, name="ffn_down_" + tag)


def kernel(x, c, ctx, c_ctx, ada_w, ada_b, norm_pre_mix, norm_post_mix, norm_pre_ffn, norm_post_ffn, ffn_up, ffn_conv_w, ffn_conv_b, ffn_down, fn_in, fn_out, attn_in, attn_q_gain, attn_k_gain, attn_out, ssm_in, ssm_conv_w, ssm_conv_b, ssm_dt_bias, ssm_a_log, ssm_d, ssm_norm, ssm_out, hy_in, hy_conv_w, hy_conv_b, hy_f_w1, hy_f_b1, hy_f_w2, hy_f_b2, hy_f_w3, hy_f_freq, hy_decay, hy_f_bias, hy_out):
    b, t, d = x.shape
    tc = ctx.shape[1]
    depth = ada_w.shape[0]
    readers = [i for i in range(depth) if i % N_MIXERS in CTX_READER_KINDS]
    last_reader = max(readers) if readers else -1

    n_rows = -(-(b + 1) // 8) * 8
    c_rows = jnp.zeros((n_rows, d), F32).at[:b].set(c).at[b].set(c_ctx)
    mod = _modulation_all(c_rows, ada_w, ada_b).reshape(depth, n_rows, 6, d)

    h, hc = x, ctx
    for i in range(depth):
        kind, j = i % N_MIXERS, i // N_MIXERS
        ctx_live = i <= last_reader
        ctx_next = i < last_reader
        m_lat = [mod[i, :b, s][:, None, :] for s in range(6)]
        sh1, sc1, g1, sh2, sc2, g2 = m_lat
        if ctx_live:
            m_ctx = [jnp.broadcast_to(mod[i, b, s][None, None, :], (b, 1, d)) for s in range(6)]
            csh1, csc1, cg1, csh2, csc2, cg2 = m_ctx
        nw_pre, nw_post = norm_pre_mix[i], norm_post_mix[i]
        if kind == 0:
            gdim = d // FN_GROUPS
            cc, scs = _fourier_chan_tables(gdim)
            w_fold = _fold_channel_dft(fn_in[j], cc, scs)
            w_out_bf = fn_out[j].astype(BF16)
            h = _fourier_mixer(h, nw_pre, sc1, sh1, w_fold, _fourier_pos_tables(t), w_out_bf, g1, nw_post, "lat")
            if ctx_next:
                hc = _fourier_mixer(hc, nw_pre, csc1, csh1, w_fold, _fourier_pos_tables(tc), w_out_bf, cg1,
                                    nw_post, "ctx")
        elif kind == 1:
            w_in_bf = attn_in[j].astype(BF16)
            w_out_bf = attn_out[j].astype(BF16)
            n_q, n_kv = d // HEAD_DIM, N_KV_HEADS
            gains = jnp.concatenate([jnp.tile(attn_q_gain[j], n_q), jnp.tile(attn_k_gain[j], n_kv)]).reshape(1, -1)
            qkv = _attn_project(h, nw_pre, sc1, sh1, w_in_bf, gains, _rope_tables(t), "lat")
            qkv_c = _attn_project(hc, nw_pre, csc1, csh1, w_in_bf, gains, None, "ctx")
            o = _attention(qkv, qkv_c, d)
            h = _outproj(o, w_out_bf, h, g1, nw_post, name="attn_out_lat")
            if ctx_next:
                oc = _attention(None, qkv_c, d)
                hc = _outproj(oc, w_out_bf, hc, cg1, nw_post, name="attn_out_ctx")
        elif kind == 2:
            w = ssm_in[j]
            d_inner = ssm_out.shape[1]
            n_conv = ssm_conv_w.shape[2]
            p = dict(wz=w[:, :d_inner].astype(BF16), wx=w[:, d_inner:d_inner + n_conv].astype(BF16),
                     wdt=w[:, d_inner + n_conv:].astype(BF16), cw=ssm_conv_w[j], cb=ssm_conv_b[j].reshape(1, -1),
                     dt_bias=ssm_dt_bias[j], a_log=ssm_a_log[j], d_skip=ssm_d[j], norm_w=ssm_norm[j],
                     w_out=ssm_out[j].astype(BF16))
            if ctx_next:
                raise NotImplementedError("context output of the SSD mixer is not needed at this depth")
            h = _ssd_mixer(h, hc, (sc1, sh1), (csc1, csh1), nw_pre, p, g1, nw_post)
        else:
            pad_r = LANES - hy_f_w1.shape[1]
            pad_c = LANES - hy_f_w1.shape[2]
            p = dict(w_in=hy_in[j].astype(BF16), cw=hy_conv_w[j], cb=hy_conv_b[j].reshape(1, -1),
                     fw1=jnp.pad(hy_f_w1[j], ((0, pad_r), (0, pad_c))),
                     fb1=jnp.pad(hy_f_b1[j], (0, pad_c)).reshape(1, LANES),
                     fw2=jnp.pad(hy_f_w2[j], ((0, pad_c), (0, pad_c))),
                     fb2=jnp.pad(hy_f_b2[j], (0, pad_c)).reshape(1, LANES),
                     ffreq=jnp.pad(hy_f_freq[j], (0, pad_c)).reshape(1, LANES),
                     fw3=jnp.pad(hy_f_w3[j], ((0, pad_c), (0, 0))), decay=hy_decay[j].reshape(1, -1),
                     fbias=hy_f_bias[j], w_out=hy_out[j].astype(BF16))
            if ctx_next:
                raise NotImplementedError("context output of the Hyena mixer is not needed at this depth")
            h = _hyena_mixer(h, (sc1, sh1), nw_pre, p, g1, nw_post)
        w_up_bf = ffn_up[i].astype(BF16)
        w_down_bf = ffn_down[i].astype(BF16)
        h = _conv_ffn(h, sc2, sh2, g2, norm_pre_ffn[i], norm_post_ffn[i], w_up_bf, ffn_conv_w[i], ffn_conv_b[i],
                      w_down_bf, "lat")
        if ctx_next:
            hc = _conv_ffn(hc, csc2, csh2, cg2, norm_pre_ffn[i], norm_post_ffn[i], w_up_bf, ffn_conv_w[i],
                           ffn_conv_b[i], w_down_bf, "ctx")
    return h
```

```python
import functools
import math

import jax
import jax.numpy as jnp
from jax import lax
from jax.experimental import pallas as pl
from jax.experimental.pallas import tpu as pltpu

F32 = jnp.float32
BF16 = jnp.bfloat16

GRID_W = 64
N_MIXERS = 4
CTX_READER_KINDS = (1, 2)
RMS_EPS = 1e-6
FN_GROUPS = 4
HEAD_DIM = 128
N_KV_HEADS = 4
ROPE_THETA = 10000.0
ROPE_FREQS = HEAD_DIM // 4
ATTN_SCALE = HEAD_DIM ** -0.5
SSM_HEAD_DIM = 64
SSM_GROUPS = 8
SSM_STATE = 128
SSM_CHUNK = 128
SSD_GROUPS_PER_STEP = 8
HY_ORDER = 2
HY_EMB = 33
HY_BANDS = (HY_EMB - 1) // 2

V7X_VMEM_LIMIT_BYTES = 56 * 1024 * 1024
LANES = 128
CONV_HALO = 16
PROJ_ROW_CHUNK = 128
PROJ_SUB_COLS = 512
NEG_BIG = -1e30


def _cparams(*sem):
    return pltpu.CompilerParams(dimension_semantics=sem, vmem_limit_bytes=V7X_VMEM_LIMIT_BYTES)


def _tile(n, pref, mult=LANES):
    if n <= pref:
        return n
    t = (pref // mult) * mult
    while t >= mult:
        if n % t == 0:
            return t
        t -= mult
    return n


def _silu(x):
    return x * (1.0 / (1.0 + jnp.exp(-x)))


def _split_bf16(v):
    hi = v.astype(BF16)
    lo = (v - hi.astype(F32)).astype(BF16)
    return hi, lo


def _dot(a, b):
    return jnp.dot(a, b, preferred_element_type=F32)


def _dot_nt(a, b):
    return lax.dot_general(a, b, (((1,), (1,)), ((), ())), preferred_element_type=F32)


def _dot3(a, b):
    ah, al = _split_bf16(a)
    bh, bl = _split_bf16(b)
    return _dot(ah, bh) + _dot(ah, bl) + _dot(al, bh)


def _rms_rows(x, w):
    return x * lax.rsqrt(jnp.mean(x * x, axis=-1, keepdims=True) + RMS_EPS) * w


def _mod_kernel(c_ref, w_ref, b_ref, o_ref):
    a = _silu(c_ref[...]).astype(BF16)
    o_ref[0] = _dot(a, w_ref[0].astype(BF16)) + b_ref[0]


def _modulation_all(c_rows, ada_w, ada_b):
    depth, d, n6 = ada_w.shape
    r = c_rows.shape[0]
    tn = _tile(n6, 1024)
    return pl.pallas_call(
        _mod_kernel,
        grid=(depth, n6 // tn),
        in_specs=[
            pl.BlockSpec((r, d), lambda l, j: (0, 0)),
            pl.BlockSpec((1, d, tn), lambda l, j: (l, 0, j)),
            pl.BlockSpec((1, 1, tn), lambda l, j: (l, 0, j)),
        ],
        out_specs=pl.BlockSpec((1, r, tn), lambda l, j: (l, 0, j)),
        out_shape=jax.ShapeDtypeStruct((depth, r, n6), F32),
        compiler_params=_cparams("parallel", "parallel"),
        name="ada_mod",
    )(c_rows, ada_w, ada_b.reshape(depth, 1, n6))


def _proj_kernel(*refs, n_w, use_conv, n_aux, n_out, epi, tm, n_row_tiles, n_col_tiles, parity):
    refs = list(refs)
    h_ref = refs.pop(0)
    if use_conv:
        hp_ref = refs.pop(0)
        hn_ref = refs.pop(0)
    nw_ref, sc_ref, sh_ref = refs.pop(0), refs.pop(0), refs.pop(0)
    w_refs = [refs.pop(0) for _ in range(n_w)]
    cw_refs, cb_refs = [], []
    if use_conv:
        for _ in range(n_w):
            cw_refs.append(refs.pop(0))
            cb_refs.append(refs.pop(0))
    aux_refs = [refs.pop(0) for _ in range(n_aux)]
    out_refs = [refs.pop(0) for _ in range(n_out)]
    u_ref, pa_ref, pb_ref = refs
    s = pl.program_id(1)
    n_pairs = n_row_tiles * n_col_tiles
    cur = jnp.minimum(s, n_pairs - 1)
    i = cur // n_col_tiles
    j = cur % n_col_tiles
    j_prev = jnp.maximum(s - 1, 0) % n_col_tiles
    hl = CONV_HALO
    off = hl if use_conv else 0
    rows_all = tm + 2 * off
    rc = _tile(tm, PROJ_ROW_CHUNK, 16)
    tn = w_refs[0].shape[1]
    spt = tn // LANES

    @pl.when((j == 0) & (s < n_pairs))
    def _():
        wv = nw_ref[...] * (1.0 + sc_ref[0])
        shv = sh_ref[0]

        def nm(x):
            return x * lax.rsqrt(jnp.mean(x * x, axis=-1, keepdims=True) + RMS_EPS) * wv + shv

        x = h_ref[0]
        inv = lax.rsqrt(jnp.mean(x * x, axis=-1, keepdims=True) + RMS_EPS)
        for r0 in range(0, tm, rc):
            u_ref[off + r0:off + r0 + rc, :] = (h_ref[0, r0:r0 + rc, :] * inv[r0:r0 + rc] * wv + shv).astype(BF16)
        if use_conv:
            u_ref[0:hl, :] = jnp.where(i == 0, 0.0, nm(hp_ref[0])).astype(BF16)
            u_ref[hl + tm:, :] = jnp.where(i == n_row_tiles - 1, 0.0, nm(hn_ref[0])).astype(BF16)

    @pl.when(s == 0)
    def _():
        pb_ref[...] = jnp.zeros(pb_ref.shape, F32)

    def body(store_ref, load_ref):
        def multiply():
            for k in range(n_w):
                p = _dot(u_ref[...], w_refs[k][...])
                for sl in range(spt):
                    store_ref[k, sl] = p[:, sl * LANES:(sl + 1) * LANES]

        def finish(row_fn, n_rows, outs):
            ck = _tile(n_rows, PROJ_ROW_CHUNK, 8) if use_conv else n_rows
            for r0 in range(0, n_rows, ck):
                for sl in range(spt):
                    lanes = slice(sl * LANES, (sl + 1) * LANES)
                    ys = []
                    for k in range(n_w):
                        before, here, after = row_fn(k, sl, r0, ck)
                        if use_conv:
                            cw = cw_refs[k][:, lanes]
                            ys.append(cw[0:1] * before + cw[1:2] * here + cw[2:3] * after + cb_refs[k][:, lanes])
                        else:
                            ys.append(here)
                    epi(j_prev, ys, aux_refs, outs, slice(r0, r0 + ck), lanes)

        if parity:
            for par in (0, 1):
                def rows(k, sl, r0, ck, par=par):
                    ld = lambda shift: load_ref[k, sl, pl.ds(off + par + shift + 2 * r0, ck, stride=2), :]
                    return (ld(-1), ld(0), ld(1)) if use_conv else (None, ld(0), None)
                finish(rows, tm // 2, [o.at[par] for o in out_refs])
        else:
            def rows(k, sl, r0, ck):
                if not use_conv:
                    return None, load_ref[k, sl, r0:r0 + ck, :], None
                pv = load_ref[k, sl, hl + r0 - 8:hl + r0 + ck + 8, :]
                return (pltpu.roll(pv, 1, axis=0)[8:8 + ck], pv[8:8 + ck],
                        pltpu.roll(pv, ck + 15, axis=0)[8:8 + ck])
            finish(rows, tm, out_refs)
        multiply()

    @pl.when(s % 2 == 0)
    def _():
        body(pa_ref, pb_ref)

    @pl.when(s % 2 == 1)
    def _():
        body(pb_ref, pa_ref)


def _proj(h, nw, sc, sh, w_views, *, n_col_tiles, tn, tm, epi, out_shapes, out_specs,
          conv=None, aux=(), aux_specs=(), parity=False, name):
    b, t, d = h.shape
    tm = _tile(t, tm, 16)
    n_row_tiles = t // tm
    n_pairs = n_row_tiles * n_col_tiles
    use_conv = conv is not None
    hl = CONV_HALO
    cur_i = lambda s: jnp.minimum(s, n_pairs - 1) // n_col_tiles
    cur_j = lambda s: jnp.minimum(s, n_pairs - 1) % n_col_tiles
    prv_i = lambda s: jnp.maximum(s - 1, 0) // n_col_tiles
    prv_j = lambda s: jnp.maximum(s - 1, 0) % n_col_tiles

    def finished(spec):
        return pl.BlockSpec(spec.block_shape, lambda bi, s, f=spec.index_map: f(bi, prv_i(s), prv_j(s)))

    ins = [h]
    specs = [pl.BlockSpec((1, tm, d), lambda bi, s: (bi, cur_i(s), 0))]
    if use_conv:
        r = tm // hl
        last = t // hl - 1
        ins += [h, h]
        specs += [
            pl.BlockSpec((1, hl, d), lambda bi, s: (bi, jnp.maximum(cur_i(s) * r - 1, 0), 0)),
            pl.BlockSpec((1, hl, d), lambda bi, s: (bi, jnp.minimum((cur_i(s) + 1) * r, last), 0)),
        ]
    ins += [nw.reshape(1, d), sc, sh]
    specs += [
        pl.BlockSpec((1, d), lambda bi, s: (0, 0)),
        pl.BlockSpec((1, 1, d), lambda bi, s: (bi, 0, 0)),
        pl.BlockSpec((1, 1, d), lambda bi, s: (bi, 0, 0)),
    ]
    for w, off in w_views:
        ins.append(w)
        specs.append(pl.BlockSpec((d, tn), lambda bi, s, off=off: (0, off + cur_j(s))))
    if use_conv:
        cw, cb = conv
        for _, off in w_views:
            ins += [cw, cb]
            specs += [
                pl.BlockSpec((cw.shape[0], tn), lambda bi, s, off=off: (0, off + prv_j(s))),
                pl.BlockSpec((1, tn), lambda bi, s, off=off: (0, off + prv_j(s))),
            ]
    ins += list(aux)
    specs += [finished(sp) for sp in aux_specs]
    rows = tm + 2 * hl if use_conv else tm
    park = (len(w_views), tn // LANES, rows, LANES)
    scratch = [pltpu.VMEM((rows, d), BF16), pltpu.VMEM(park, F32), pltpu.VMEM(park, F32)]
    kern = functools.partial(_proj_kernel, n_w=len(w_views), use_conv=use_conv, n_aux=len(aux),
                             n_out=len(out_shapes), epi=epi, tm=tm, n_row_tiles=n_row_tiles,
                             n_col_tiles=n_col_tiles, parity=parity)
    return pl.pallas_call(
        kern,
        grid=(b, n_pairs + 1),
        in_specs=specs,
        out_specs=[finished(sp) for sp in out_specs(tm)],
        out_shape=out_shapes,
        scratch_shapes=scratch,
        compiler_params=_cparams("parallel", "arbitrary"),
        name=name,
    )(*ins)


def _std_out(b, t, n, tn, dtype):
    shapes = [jax.ShapeDtypeStruct((b, t, n), dtype)]
    specs = lambda tm: [pl.BlockSpec((1, tm, tn), lambda bi, i, j: (bi, i, j))]
    return shapes, specs


def _parity_out(b, t, n, tn, dtypes):
    shapes = [jax.ShapeDtypeStruct((2, b, t // 2, n), dt) for dt in dtypes]
    specs = lambda tm: [pl.BlockSpec((2, 1, tm // 2, tn), lambda bi, i, j: (0, bi, i, j)) for _ in dtypes]
    return shapes, specs


def _epi_identity(j, ys, aux_refs, out_refs, rows, cols):
    for o_ref in out_refs:
        o_ref[0, rows, cols] = ys[0].astype(o_ref.dtype)


def _epi_silu(j, ys, aux_refs, out_refs, rows, cols):
    out_refs[0][0, rows, cols] = _silu(ys[0]).astype(out_refs[0].dtype)


def _epi_swiglu(j, ys, aux_refs, out_refs, rows, cols):
    out_refs[0][0, rows, cols] = (_silu(ys[0]) * ys[1]).astype(out_refs[0].dtype)


def _outproj_kernel(a_ref, w_ref, h_ref, g_ref, nw_ref, o_ref, acc_ref, *, nj, tn):
    j = pl.program_id(2)
    acc_ref[j] = _dot(a_ref[0], w_ref[...])

    @pl.when(j == nj - 1)
    def _():
        ss = None
        for jj in range(nj):
            o = acc_ref[jj]
            s = jnp.sum(o * o, axis=-1, keepdims=True)
            ss = s if ss is None else ss + s
        inv = lax.rsqrt(ss * (1.0 / (nj * tn)) + RMS_EPS)
        for jj in range(nj):
            cols = slice(jj * tn, (jj + 1) * tn)
            o_ref[0, :, cols] = h_ref[0, :, cols] + g_ref[0, :, cols] * (acc_ref[jj] * inv * nw_ref[:, cols])


def _outproj_parity_kernel(a_ref, w_ref, h_ref, g_ref, nw_ref, o_ref, acc_ref, *, nj, tn, hm):
    j = pl.program_id(2)
    spt = tn // LANES
    for par in (0, 1):
        r = _dot(a_ref[par, 0], w_ref[...])
        for sl in range(spt):
            acc_ref[j * spt + sl, pl.ds(par, hm, stride=2), :] = r[:, sl * LANES:(sl + 1) * LANES]

    @pl.when(j == nj - 1)
    def _():
        ss = None
        for sl in range(nj * spt):
            o = acc_ref[sl]
            s = jnp.sum(o * o, axis=-1, keepdims=True)
            ss = s if ss is None else ss + s
        inv = lax.rsqrt(ss * (1.0 / (nj * tn)) + RMS_EPS)
        for sl in range(nj * spt):
            cols = slice(sl * LANES, (sl + 1) * LANES)
            o_ref[0, :, cols] = h_ref[0, :, cols] + g_ref[0, :, cols] * (acc_ref[sl] * inv * nw_ref[:, cols])


def _outproj(a, w, h, g, nw, *, tm=512, tn=512, parity=False, name):
    b, t, d = h.shape
    kdim = w.shape[0]
    tm = _tile(t, tm, 16)
    tn = _tile(d, tn)
    nj = d // tn
    if parity:
        hm = tm // 2
        return pl.pallas_call(
            functools.partial(_outproj_parity_kernel, nj=nj, tn=tn, hm=hm),
            grid=(b, t // tm, nj),
            in_specs=[
                pl.BlockSpec((2, 1, hm, kdim), lambda bi, i, j: (0, bi, i, 0)),
                pl.BlockSpec((kdim, tn), lambda bi, i, j: (0, j)),
                pl.BlockSpec((1, tm, d), lambda bi, i, j: (bi, i, 0)),
                pl.BlockSpec((1, 1, d), lambda bi, i, j: (bi, 0, 0)),
                pl.BlockSpec((1, d), lambda bi, i, j: (0, 0)),
            ],
            out_specs=pl.BlockSpec((1, tm, d), lambda bi, i, j: (bi, i, 0)),
            out_shape=jax.ShapeDtypeStruct((b, t, d), F32),
            scratch_shapes=[pltpu.VMEM((d // LANES, tm, LANES), F32)],
            compiler_params=_cparams("parallel", "parallel", "arbitrary"),
            name=name,
        )(a, w, h, g, nw.reshape(1, d))
    return pl.pallas_call(
        functools.partial(_outproj_kernel, nj=nj, tn=tn),
        grid=(b, t // tm, nj),
        in_specs=[
            pl.BlockSpec((1, tm, kdim), lambda bi, i, j: (bi, i, 0)),
            pl.BlockSpec((kdim, tn), lambda bi, i, j: (0, j)),
            pl.BlockSpec((1, tm, d), lambda bi, i, j: (bi, i, 0)),
            pl.BlockSpec((1, 1, d), lambda bi, i, j: (bi, 0, 0)),
            pl.BlockSpec((1, d), lambda bi, i, j: (0, 0)),
        ],
        out_specs=pl.BlockSpec((1, tm, d), lambda bi, i, j: (bi, i, 0)),
        out_shape=jax.ShapeDtypeStruct((b, t, d), F32),
        scratch_shapes=[pltpu.VMEM((nj, tm, tn), F32)],
        compiler_params=_cparams("parallel", "parallel", "arbitrary"),
        name=name,
    )(a, w, h, g, nw.reshape(1, d))


def _dft_kernel(*refs, n_l, n_r, n_acc, terms, n_aux, n_out, epi):
    refs = list(refs)
    l_refs = [refs.pop(0) for _ in range(n_l)]
    r_refs = [refs.pop(0) for _ in range(n_r)]
    aux_refs = [refs.pop(0) for _ in range(n_aux)]
    out_refs = [refs.pop(0) for _ in range(n_out)]
    accs = [None] * n_acc
    for o, l, r in terms:
        dd = _dot(l_refs[l][0], r_refs[r][0])
        accs[o] = dd if accs[o] is None else accs[o] + dd
    epi(pl.program_id(1), accs, aux_refs, out_refs)


def _dft_call(l_tabs, r_views, terms, n_acc, *, n_par, batch, m, kdim, n, tm, tn, epi, out_shapes, out_specs,
              aux=(), aux_specs=None, name):
    tm = _tile(m, tm, 16)
    tn = _tile(n, tn)
    ins, specs = [], []
    for lt in l_tabs:
        ins.append(lt)
        specs.append(pl.BlockSpec((1, tm, kdim), lambda p, i, j, b: (p, i, 0)))
    for arr, lead_fn, col_fn in r_views:
        ins.append(arr)
        specs.append(pl.BlockSpec((1, kdim, tn),
                                  lambda p, i, j, b, lf=lead_fn, cf=col_fn: (lf(p, b), 0, cf(p, j))))
    ins += list(aux)
    if aux_specs is not None:
        specs += list(aux_specs(tm, tn))
    kern = functools.partial(_dft_kernel, n_l=len(l_tabs), n_r=len(r_views), n_acc=n_acc, terms=terms,
                             n_aux=len(aux), n_out=len(out_shapes), epi=epi)
    return pl.pallas_call(
        kern,
        grid=(n_par, m // tm, n // tn, batch),
        in_specs=specs,
        out_specs=out_specs(tm, tn),
        out_shape=out_shapes,
        compiler_params=_cparams("parallel", "parallel", "parallel", "parallel"),
        name=name,
    )(*ins)


def _int_grid(n):
    k = lax.broadcasted_iota(jnp.int32, (n, n), 0)
    j = lax.broadcasted_iota(jnp.int32, (n, n), 1)
    return k, j


def _fourier_pos_tables(n):
    k, j = _int_grid(n // 2)
    tabs = []
    for par in (0, 1):
        ang = ((k * (2 * j + par)) % n).astype(F32) * (2.0 * math.pi / n)
        tabs += [jnp.cos(ang).astype(BF16)[None], (-jnp.sin(ang)).astype(BF16)[None]]
    return tabs


def _fourier_chan_tables(gdim):
    kc, jc = _int_grid(gdim)
    angc = ((kc * jc) % gdim).astype(F32) * (2.0 * math.pi / gdim)
    return jnp.cos(angc), jnp.sin(angc)


def _wfold_kernel(w_ref, c_ref, s_ref, oc_ref, os_ref):
    w = w_ref[...]
    oc_ref[...] = _dot3(w, c_ref[...]).astype(BF16)
    os_ref[...] = _dot3(w, s_ref[...]).astype(BF16)


def _fold_channel_dft(w_in, cc, sc):
    d = w_in.shape[0]
    gdim = cc.shape[0]
    ng = d // gdim
    oc, os_ = pl.pallas_call(
        _wfold_kernel,
        grid=(ng,),
        in_specs=[
            pl.BlockSpec((d, gdim), lambda g: (0, g)),
            pl.BlockSpec((gdim, gdim), lambda g: (0, 0)),
            pl.BlockSpec((gdim, gdim), lambda g: (0, 0)),
        ],
        out_specs=[pl.BlockSpec((d, gdim), lambda g: (0, g)), pl.BlockSpec((d, gdim), lambda g: (0, g))],
        out_shape=[jax.ShapeDtypeStruct((d, d), BF16), jax.ShapeDtypeStruct((d, d), BF16)],
        compiler_params=_cparams("parallel"),
        name="fourier_fold",
    )(w_in, cc, sc)
    return jnp.concatenate([oc, os_], axis=1)


def _fourier_mixer(h, nw, sc, sh, w_fold, tables, w_out_bf, g, nw_post, tag):
    b, t, d = h.shape
    half = t // 2
    tn = _tile(2 * d, 512)
    shapes, specs = _parity_out(b, t, 2 * d, tn, [BF16])
    a = _proj(h, nw, sc, sh, [(w_fold, 0)], n_col_tiles=2 * d // tn, tn=tn, tm=1024, epi=_epi_identity,
              out_shapes=shapes, out_specs=specs, parity=True, name="fourier_in_" + tag)[0]
    scale = 1.0 / math.sqrt(t * (d // FN_GROUPS))

    def epi(i, accs, aux_refs, out_refs):
        ev, od = accs
        out_refs[0][0, 0] = ((ev + od) * scale).astype(BF16)
        out_refs[0][0, 1] = ((ev - od) * scale).astype(BF16)

    a2 = a.reshape(2 * b, half, 2 * d)
    tn2 = _tile(d, 512)
    nj = d // tn2
    views = [(a2, lambda p, bi, par=par: par * b + bi, lambda p, j, q=q: q * nj + j)
             for par in (0, 1) for q in (0, 1)]
    f = _dft_call(
        tables, views, [(0, 0, 0), (0, 1, 1), (1, 2, 2), (1, 3, 3)], 2,
        n_par=1, batch=b, m=half, kdim=half, n=d, tm=512, tn=tn2, epi=epi,
        out_shapes=[jax.ShapeDtypeStruct((b, 2, half, d), BF16)],
        out_specs=lambda tm, tn_: [pl.BlockSpec((1, 2, tm, tn_), lambda p, i, j, bi: (bi, 0, i, j))],
        name="fourier_pos_" + tag)[0]
    return _outproj(f.reshape(b, t, d), w_out_bf, h, g, nw_post, name="fourier_out_" + tag)


def _rope_tables(n_tokens):
    rows = n_tokens // GRID_W
    row = jnp.broadcast_to(jnp.arange(rows, dtype=F32)[:, None], (rows, GRID_W)).reshape(n_tokens)
    col = jnp.broadcast_to(jnp.arange(GRID_W, dtype=F32)[None, :], (rows, GRID_W)).reshape(n_tokens)
    inv_freq = ROPE_THETA ** (-jnp.arange(ROPE_FREQS, dtype=F32) / ROPE_FREQS)
    ar, ac = row[:, None] * inv_freq, col[:, None] * inv_freq
    cos = jnp.concatenate([jnp.cos(ar), jnp.cos(ar), jnp.cos(ac), jnp.cos(ac)], axis=1)
    sin = jnp.concatenate([-jnp.sin(ar), jnp.sin(ar), -jnp.sin(ac), jnp.sin(ac)], axis=1)
    return cos, sin


def _attn_in_epi(j, ys, aux_refs, out_refs, rows, cols, *, n_q_tiles, rope):
    y = ys[0]
    o_ref = out_refs[0]
    gain = aux_refs[0][:, cols] * jnp.where(j < n_q_tiles, ATTN_SCALE, 1.0)
    t = y * lax.rsqrt(jnp.mean(y * y, axis=-1, keepdims=True) + RMS_EPS) * gain
    if rope:
        cos, sin = aux_refs[1][rows, :], aux_refs[2][rows, :]
        lane = lax.broadcasted_iota(jnp.int32, t.shape, 1)
        half = ROPE_FREQS
        partner = jnp.where(lane % (2 * half) < half,
                            pltpu.roll(t, HEAD_DIM - half, axis=1), pltpu.roll(t, half, axis=1))
        t = t * cos + partner * sin
    o_ref[0, rows, cols] = jnp.where(j <= n_q_tiles, t, y).astype(o_ref.dtype)


def _attn_project(h, nw, sc, sh, w_in_bf, gains, rope_tabs, tag):
    b, t, d = h.shape
    n = w_in_bf.shape[1]
    tn = N_KV_HEADS * HEAD_DIM
    n_q_tiles = d // tn
    rope = rope_tabs is not None
    aux = [gains]
    aux_specs = [pl.BlockSpec((1, tn), lambda bi, i, j: (0, jnp.minimum(j, n_q_tiles)))]
    tm = _tile(t, 1024, 16)
    if rope:
        aux += list(rope_tabs)
        aux_specs += [pl.BlockSpec((tm, HEAD_DIM), lambda bi, i, j: (i, 0))] * 2
    epi = functools.partial(_attn_in_epi, n_q_tiles=n_q_tiles, rope=rope)
    shapes, specs = _std_out(b, t, n, tn, BF16)
    return _proj(h, nw, sc, sh, [(w_in_bf, 0)], n_col_tiles=n // tn, tn=tn, tm=tm, epi=epi,
                 out_shapes=shapes, out_specs=specs, aux=aux, aux_specs=aux_specs, name="attn_in_" + tag)[0]


def _attn_kernel(*refs, has_lat, tkv, n_lat_chunks):
    if has_lat:
        q_ref, kc_ref, vc_ref, k_ref, v_ref, o_ref = refs
    else:
        q_ref, kc_ref, vc_ref, o_ref = refs
    q = q_ref[0]
    s = _dot_nt(q, kc_ref[0])
    m = jnp.max(s, axis=-1, keepdims=True)
    p = jnp.exp(s - m)
    l = jnp.sum(p, axis=-1, keepdims=True)
    acc = _dot(p.astype(BF16), vc_ref[0])
    if has_lat:
        for c in range(n_lat_chunks):
            kk = k_ref[0, c * tkv:(c + 1) * tkv, :]
            vv = v_ref[0, c * tkv:(c + 1) * tkv, :]
            s = _dot_nt(q, kk)
            m_new = jnp.maximum(m, jnp.max(s, axis=-1, keepdims=True))
            alpha = jnp.exp(m - m_new)
            p = jnp.exp(s - m_new)
            l = alpha * l + jnp.sum(p, axis=-1, keepdims=True)
            acc = alpha * acc + _dot(p.astype(BF16), vv)
            m = m_new
    o_ref[0] = (acc / l).astype(o_ref.dtype)


def _attention(qkv, qkv_ctx, d, *, tq=512, tkv=1024):
    has_lat = qkv is not None
    src = qkv if has_lat else qkv_ctx
    b, t, _ = src.shape
    tc = qkv_ctx.shape[1]
    n_heads = d // HEAD_DIM
    grp = n_heads // N_KV_HEADS
    k_off = n_heads
    v_off = n_heads + N_KV_HEADS
    tq = _tile(t, tq, 16)
    tkv = _tile(t, tkv)
    ins = [src, qkv_ctx, qkv_ctx]
    specs = [
        pl.BlockSpec((1, tq, HEAD_DIM), lambda bi, hh, i: (bi, i, hh)),
        pl.BlockSpec((1, tc, HEAD_DIM), lambda bi, hh, i: (bi, 0, k_off + hh // grp)),
        pl.BlockSpec((1, tc, HEAD_DIM), lambda bi, hh, i: (bi, 0, v_off + hh // grp)),
    ]
    if has_lat:
        ins += [qkv, qkv]
        specs += [
            pl.BlockSpec((1, t, HEAD_DIM), lambda bi, hh, i: (bi, 0, k_off + hh // grp)),
            pl.BlockSpec((1, t, HEAD_DIM), lambda bi, hh, i: (bi, 0, v_off + hh // grp)),
        ]
    return pl.pallas_call(
        functools.partial(_attn_kernel, has_lat=has_lat, tkv=tkv, n_lat_chunks=t // tkv),
        grid=(b, n_heads, t // tq),
        in_specs=specs,
        out_specs=pl.BlockSpec((1, tq, HEAD_DIM), lambda bi, hh, i: (bi, i, hh)),
        out_shape=jax.ShapeDtypeStruct((b, t, d), BF16),
        compiler_params=_cparams("parallel", "parallel", "parallel"),
        name="attn_core_lat" if has_lat else "attn_core_ctx",
    )(*ins)


def _dt_epi(j, ys, aux_refs, out_refs, rows, cols):
    x = ys[0] + aux_refs[0][:, cols]
    dt = jnp.maximum(x, 0.0) + jnp.log(1.0 + jnp.exp(-jnp.abs(x)))
    out_refs[0][0, rows, cols] = dt
    out_refs[1][0, rows, cols] = dt * (-jnp.exp(aux_refs[1][:, cols]))


def _ssd_kernel(*refs, rev, with_output, final, nc):
    refs = list(refs)
    x_ref, b_ref, c_ref, dt_ref, adt_ref, s0_ref = [refs.pop(0) for _ in range(6)]
    if final:
        yprev_ref, z_ref, dskip_ref, nw_ref = [refs.pop(0) for _ in range(4)]
    y_ref = refs.pop(0) if with_output else None
    sfin_ref = refs.pop(0)
    st_ref = refs.pop(0)
    c = pl.program_id(2)
    q = SSM_CHUNK
    hp = SSM_HEAD_DIM
    ns = SSM_STATE
    gpb = dt_ref.shape[1]
    n_hg = dt_ref.shape[2]
    width = n_hg * hp

    @pl.when(c == 0)
    def _():
        st_ref[...] = s0_ref[0]

    ii = lax.broadcasted_iota(jnp.int32, (q, q), 0)
    jj = lax.broadcasted_iota(jnp.int32, (q, q), 1)
    mask = (jj >= ii) if rev else (jj <= ii)
    mask_bf = mask.astype(F32).astype(BF16)
    mask_t_bf = ((ii >= jj) if rev else (ii <= jj)).astype(F32).astype(BF16)
    eye_bf = (ii == jj).astype(F32).astype(BF16)
    lane = lax.broadcasted_iota(jnp.int32, (q, 2 * hp), 1)

    def hi_lo(v):
        hi = v.astype(BF16).astype(F32)
        return hi, v - hi

    def expand_rows(v):
        return jnp.concatenate([jnp.broadcast_to(v[hh:hh + 1, :], (hp, q)) for hh in range(n_hg)], axis=0)

    groups = range(gpb)
    cols = [slice(gi * width, (gi + 1) * width) for gi in groups]
    scols = [slice(gi * ns, (gi + 1) * ns) for gi in groups]
    dt_r = [dt_ref[0, gi] for gi in groups]
    a_parts = [hi_lo(adt_ref[0, gi]) for gi in groups]
    d_parts = [hi_lo(dt_r[gi]) for gi in groups]
    e_acs = [_dot_nt(mask_bf, expand_rows(a_parts[gi][0]).astype(BF16))
             + _dot_nt(mask_bf, expand_rows(a_parts[gi][1]).astype(BF16)) for gi in groups]
    e_dt = [_dot_nt(eye_bf, expand_rows(d_parts[gi][0]).astype(BF16))
            + _dot_nt(eye_bf, expand_rows(d_parts[gi][1]).astype(BF16)) for gi in groups]
    last = [e_acs[gi][0:1, :] if rev else e_acs[gi][q - 1:q, :] for gi in groups]

    x = [x_ref[0, :, cols[gi]] for gi in groups]
    bm = [b_ref[0, :, scols[gi]] for gi in groups]
    s_old = [st_ref[gi] for gi in groups]
    for gi in groups:
        wend = jnp.exp(last[gi] - e_acs[gi]) * e_dt[gi]
        s_new = s_old[gi] * jnp.exp(last[gi]) + _dot(bm[gi].T.astype(BF16), (x[gi] * wend).astype(BF16))
        st_ref[gi] = s_new

        @pl.when(c == nc - 1)
        def _():
            sfin_ref[0, gi] = s_new

    if not with_output:
        return
    cm_bf = [c_ref[0, :, scols[gi]].astype(BF16) for gi in groups]
    cb = [_dot_nt(cm_bf[gi], bm[gi].astype(BF16)) for gi in groups]
    acs_r = []
    for gi in groups:
        r16 = _dot(jnp.concatenate(a_parts[gi], axis=0).astype(BF16), mask_t_bf)
        acs_r.append(r16[0:n_hg] + r16[n_hg:2 * n_hg])
    y_state = [_dot(cm_bf[gi], s_old[gi].astype(BF16)) * jnp.exp(e_acs[gi]) for gi in groups]
    ys = [[] for _ in groups]
    for pair in range(n_hg // 2):
        for gi in groups:
            ws = []
            for hh in (2 * pair, 2 * pair + 1):
                seg = e_acs[gi][:, hh * hp:hh * hp + 1] - acs_r[gi][hh:hh + 1, :]
                decay = jnp.exp(jnp.where(mask, seg, NEG_BIG))
                ws.append((cb[gi] * decay * dt_r[gi][hh:hh + 1, :]).astype(BF16))
            xp = x[gi][:, pair * 2 * hp:(pair + 1) * 2 * hp]
            bd = jnp.concatenate([jnp.where(lane < hp, xp, 0.0), jnp.where(lane >= hp, xp, 0.0)], axis=0)
            ys[gi].append(_dot(jnp.concatenate(ws, axis=1), bd.astype(BF16)))
    for gi in groups:
        y = jnp.concatenate(ys[gi], axis=1) + y_state[gi]
        if final:
            yt = yprev_ref[0, :, cols[gi]] + y + x[gi] * dskip_ref[:, cols[gi]]
            yt = yt * _silu(z_ref[0, :, cols[gi]])
            y_ref[0, :, cols[gi]] = _rms_rows(yt, nw_ref[:, cols[gi]]).astype(y_ref.dtype)
        else:
            y_ref[0, :, cols[gi]] = y


def _ssd_scan(xbc, dt_rows, adt_rows, s0, d_inner, *, rev, with_output, final_args=None):
    b, t, _ = xbc.shape
    g = SSM_GROUPS
    q = SSM_CHUNK
    nc = t // q
    width = d_inner // g
    n_hg = width // SSM_HEAD_DIM
    gpb = SSD_GROUPS_PER_STEP
    ngb = g // gpb
    wblk = gpb * width
    sblk = gpb * SSM_STATE
    b_off = d_inner // sblk
    c_off = b_off + ngb
    final = final_args is not None
    cmap = (lambda c: nc - 1 - c) if rev else (lambda c: c)
    ins = [xbc, xbc, xbc, dt_rows, adt_rows, s0]
    specs = [
        pl.BlockSpec((1, q, wblk), lambda bi, gi, c: (bi, cmap(c), gi)),
        pl.BlockSpec((1, q, sblk), lambda bi, gi, c: (bi, cmap(c), b_off + gi)),
        pl.BlockSpec((1, q, sblk), lambda bi, gi, c: (bi, cmap(c), c_off + gi)),
        pl.BlockSpec((1, gpb, n_hg, q), lambda bi, gi, c: (bi, gi, 0, cmap(c))),
        pl.BlockSpec((1, gpb, n_hg, q), lambda bi, gi, c: (bi, gi, 0, cmap(c))),
        pl.BlockSpec((1, gpb, SSM_STATE, width), lambda bi, gi, c: (bi, gi, 0, 0)),
    ]
    if final:
        yprev, z, dskip, nw = final_args
        ins += [yprev, z, dskip, nw]
        specs += [
            pl.BlockSpec((1, q, wblk), lambda bi, gi, c: (bi, cmap(c), gi)),
            pl.BlockSpec((1, q, wblk), lambda bi, gi, c: (bi, cmap(c), gi)),
            pl.BlockSpec((1, wblk), lambda bi, gi, c: (0, gi)),
            pl.BlockSpec((1, wblk), lambda bi, gi, c: (0, gi)),
        ]
    out_shapes, out_specs = [], []
    if with_output:
        out_shapes.append(jax.ShapeDtypeStruct((b, t, d_inner), BF16 if final else F32))
        out_specs.append(pl.BlockSpec((1, q, wblk), lambda bi, gi, c: (bi, cmap(c), gi)))
    out_shapes.append(jax.ShapeDtypeStruct(s0.shape, F32))
    out_specs.append(pl.BlockSpec((1, gpb, SSM_STATE, width), lambda bi, gi, c: (bi, gi, 0, 0)))
    return pl.pallas_call(
        functools.partial(_ssd_kernel, rev=rev, with_output=with_output, final=final, nc=nc),
        grid=(b, ngb, nc),
        in_specs=specs,
        out_specs=out_specs,
        out_shape=out_shapes,
        scratch_shapes=[pltpu.VMEM((gpb, SSM_STATE, width), F32)],
        compiler_params=_cparams("parallel", "parallel", "arbitrary"),
        name="ssd_scan_%s%s" % ("bwd" if rev else "fwd", "" if with_output else "_state"),
    )(*ins)


def _ssd_project(h, nw, sc, sh, wz, wx, wdt, cw, cb, dt_bias, a_log, need_z, tag):
    b, t, d = h.shape
    d_inner = wz.shape[1]
    z = None
    if need_z:
        tn = _tile(d_inner, 512)
        shapes, specs = _std_out(b, t, d_inner, tn, F32)
        z = _proj(h, nw, sc, sh, [(wz, 0)], n_col_tiles=d_inner // tn, tn=tn, tm=1024, epi=_epi_identity,
                  out_shapes=shapes, out_specs=specs, name="ssd_in_z_" + tag)[0]
    nx = wx.shape[1]
    tn = _tile(nx, 512)
    shapes, specs = _std_out(b, t, nx, tn, F32)
    xbc = _proj(h, nw, sc, sh, [(wx, 0)], n_col_tiles=nx // tn, tn=tn, tm=1024, epi=_epi_silu,
                conv=(cw, cb), out_shapes=shapes, out_specs=specs, name="ssd_in_xbc_" + tag)[0]
    nd = wdt.shape[1]
    shapes = [jax.ShapeDtypeStruct((b, t, nd), F32)] * 2
    specs = lambda tm: [pl.BlockSpec((1, tm, nd), lambda bi, i, j: (bi, i, 0))] * 2
    dt, adt = _proj(h, nw, sc, sh, [(wdt, 0)], n_col_tiles=1, tn=nd, tm=1024, epi=_dt_epi,
                    out_shapes=shapes, out_specs=specs,
                    aux=[dt_bias.reshape(1, nd), a_log.reshape(1, nd)],
                    aux_specs=[pl.BlockSpec((1, nd), lambda bi, i, j: (0, 0))] * 2, name="ssd_in_dt_" + tag)

    def rows(v):
        return jnp.transpose(v.reshape(b, t, 2, SSM_GROUPS, -1), (2, 0, 3, 4, 1))

    return z, xbc, rows(dt), rows(adt)


def _ssd_mixer(h, hc, mods, mods_c, nw_pre, p, g, nw_post):
    b, t, d = h.shape
    wz, wx, wdt = p["wz"], p["wx"], p["wdt"]
    d_inner = wz.shape[1]
    sc, sh = mods
    scc, shc = mods_c
    z, xbc, dt_r, adt_r = _ssd_project(h, nw_pre, sc, sh, wz, wx, wdt, p["cw"], p["cb"], p["dt_bias"],
                                       p["a_log"], True, "lat")
    _, xbc_c, dt_rc, adt_rc = _ssd_project(hc, nw_pre, scc, shc, wz, wx, wdt, p["cw"], p["cb"], p["dt_bias"],
                                           p["a_log"], False, "ctx")
    s0 = jnp.zeros((b, SSM_GROUPS, SSM_STATE, d_inner // SSM_GROUPS), F32)
    (s_f,) = _ssd_scan(xbc_c, dt_rc[0], adt_rc[0], s0, d_inner, rev=False, with_output=False)
    (s_b,) = _ssd_scan(xbc_c, dt_rc[1], adt_rc[1], s0, d_inner, rev=True, with_output=False)
    y_f, _ = _ssd_scan(xbc, dt_r[0], adt_r[0], s_f, d_inner, rev=False, with_output=True)
    dskip = jnp.repeat(p["d_skip"], SSM_HEAD_DIM).reshape(1, d_inner)
    yn, _ = _ssd_scan(xbc, dt_r[1], adt_r[1], s_b, d_inner, rev=True, with_output=True,
                      final_args=(y_f, z, dskip, p["norm_w"].reshape(1, d_inner)))
    return _outproj(yn, p["w_out"], h, g, nw_post, name="ssd_out")


def _hyena_tables(n):
    half = n // 2
    k, s = _int_grid(half)
    def trig(m):
        ang = (m % (2 * n)).astype(F32) * (math.pi / n)
        return jnp.cos(ang), jnp.sin(ang)
    ce, se = trig(2 * s * k)
    co, so = trig((2 * s + 1) * k)
    cot, sot = trig((2 * k + 1) * s)
    alt_s = jnp.where(s % 2 == 0, 1.0, -1.0)
    alt_k = jnp.where(k % 2 == 0, 1.0, -1.0)
    sep = jnp.where(k == 0, alt_s, se)
    sop = jnp.where(k == 0, alt_s, so)
    g = jnp.where(k == 0, alt_s, -se)
    nsept = jnp.where(s == 0, -alt_k, -se)
    nsopt = jnp.where(s == 0, -alt_k, -sot)
    bf = lambda *ts: jnp.stack([t.astype(BF16) for t in ts])
    return dict(ce=bf(ce), co=bf(co), sep=bf(sep), sop=bf(sop), g=bf(g), nsop=bf(-sop),
                inv_c=bf(ce, cot), inv_s=bf(nsept, nsopt))


def _hyena_feats(n):
    t01 = jnp.linspace(0.0, 1.0, n, dtype=F32)[:, None]
    w = (2.0 * math.pi / n) * jnp.arange(n, dtype=F32)[:, None]
    f = jnp.linspace(1e-4, HY_BANDS - 1, HY_BANDS, dtype=F32)[None, :]
    feats = jnp.concatenate([t01, jnp.cos(f * w), -jnp.sin(f * w)], axis=-1)
    return jnp.pad(feats, ((0, 0), (0, LANES - HY_EMB)))


def _taps_kernel(feats_ref, w1_ref, b1_ref, w2_ref, b2_ref, fr_ref, w3_ref, dec_ref, o_ref, hid_ref, *,
                 tiles_per_dir):
    j = pl.program_id(0)

    @pl.when(j == 0)
    def _():
        fr = fr_ref[...]
        h1 = jnp.sin(fr * (_dot3(feats_ref[...], w1_ref[...]) + b1_ref[...]))
        hid_ref[...] = jnp.sin(fr * (_dot3(h1, w2_ref[...]) + b2_ref[...]))

    taps = _dot3(hid_ref[...], w3_ref[...])
    taps = taps * jnp.exp(-feats_ref[:, 0:1] * jnp.abs(dec_ref[...]))
    is_bwd = (j // tiles_per_dir) % 2 == 1
    row = lax.broadcasted_iota(jnp.int32, taps.shape, 0)
    o_ref[0] = jnp.where((row == 0) & is_bwd, 0.0, taps).astype(o_ref.dtype)


def _hyena_taps(n, p, d):
    feats = _hyena_feats(n)
    tn = _tile(d, 256)
    tpd = d // tn
    ncols = HY_ORDER * 2 * d

    def omap(j):
        return ((j // tpd) % 2, 0, (j // (2 * tpd)) * tpd + j % tpd)

    full = lambda j: (0, 0)
    return pl.pallas_call(
        functools.partial(_taps_kernel, tiles_per_dir=tpd),
        grid=(ncols // tn,),
        in_specs=[
            pl.BlockSpec((n, LANES), full),
            pl.BlockSpec((LANES, LANES), full), pl.BlockSpec((1, LANES), full),
            pl.BlockSpec((LANES, LANES), full), pl.BlockSpec((1, LANES), full),
            pl.BlockSpec((1, LANES), full),
            pl.BlockSpec((LANES, tn), lambda j: (0, j)),
            pl.BlockSpec((1, tn), lambda j: (0, j)),
        ],
        out_specs=pl.BlockSpec((1, n, tn), omap),
        out_shape=jax.ShapeDtypeStruct((2, n, HY_ORDER * d), BF16),
        scratch_shapes=[pltpu.VMEM((n, LANES), F32)],
        compiler_params=_cparams("arbitrary"),
        name="hyena_taps",
    )(feats, p["fw1"], p["fb1"], p["fw2"], p["fb2"], p["ffreq"], p["fw3"], p["decay"])


def _first_row(shape, i):
    return (lax.broadcasted_iota(jnp.int32, shape, 0) == 0) & (i == 0)


def _hyena_spectrum(taps, tabs, n, d):
    half = n // 2
    ncol = HY_ORDER * d
    taps2 = taps.reshape(2, half, 2 * ncol)
    tn = _tile(ncol, 512)
    nj = ncol // tn
    views = [(taps2, lambda p, b, dr=dr: dr, lambda p, j, par=par: par * nj + j)
             for dr in (0, 1) for par in (0, 1)]
    l_tabs = [tabs["ce"], tabs["co"], tabs["sep"], tabs["sop"], tabs["g"], tabs["nsop"]]
    terms = [(0, 0, 0), (0, 0, 2), (1, 1, 1), (1, 1, 3), (2, 2, 0), (2, 4, 2), (3, 3, 1), (3, 5, 3)]

    def epi(i, accs, aux_refs, out_refs):
        pec, poc, pes, pos = accs
        first = _first_row(pec.shape, i)
        out_refs[0][...] = pec + poc
        out_refs[1][...] = jnp.where(first, pes, -(pes + pos))
        out_refs[2][...] = pec - poc
        out_refs[3][...] = jnp.where(first, -pos, pes - pos)

    return _dft_call(
        l_tabs, views, terms, 4, n_par=1, batch=1, m=half, kdim=half, n=ncol, tm=256, tn=tn, epi=epi,
        out_shapes=[jax.ShapeDtypeStruct((half, ncol), F32)] * 4,
        out_specs=lambda tm, tn_: [pl.BlockSpec((tm, tn_), lambda p, i, j, b: (i, j))] * 4,
        name="hyena_spectrum")


def _hyena_long_conv(src_bf, src_f32, gate, gate_off, spec, order, bias, tabs, d, out_dtypes, tag):
    b2, half, _ = src_bf.shape
    b = b2 // 2
    n = 2 * half
    inv_n = 1.0 / n
    tn = _tile(d, 512)
    nj = d // tn

    def fwd_epi(i, accs, aux_refs, out_refs):
        pec, poc, pes, pos = accs
        hr, hi, hur, hui = [a[...] for a in aux_refs]
        vr, ur = pec + poc, pec - poc
        vi, ui = -(pes + pos), pes - pos
        yr, yi = vr * hr - vi * hi, vr * hi + vi * hr
        yur, yui = ur * hur - ui * hui, ur * hui + ui * hur
        first = _first_row(pec.shape, i)
        y0, yny = vr * hr, ur * hur
        mr, mi = pes, -pos
        ymr, ymi = mr * hi - mi * hui, mr * hui + mi * hi
        ae = jnp.where(first, 0.5 * (y0 + yny), yr + yur)
        be = jnp.where(first, -ymr, yi - yui)
        ao = jnp.where(first, 0.5 * (y0 - yny), yr - yur)
        bo = jnp.where(first, ymi, yi + yui)
        for q, v in enumerate((ae, be, ao, bo)):
            out_refs[0][0, :, q * tn:(q + 1) * tn] = (v * inv_n).astype(BF16)

    views = [(src_bf, lambda p, bi: bi, lambda p, j: j), (src_bf, lambda p, bi: b + bi, lambda p, j: j)]
    yab = _dft_call(
        [tabs["ce"], tabs["co"], tabs["sep"], tabs["sop"]], views,
        [(0, 0, 0), (1, 1, 1), (2, 2, 0), (3, 3, 1)], 4,
        n_par=1, batch=b, m=half, kdim=half, n=d, tm=512, tn=tn, epi=fwd_epi,
        out_shapes=[jax.ShapeDtypeStruct((b, half, 4 * d), BF16)],
        out_specs=lambda tm, tn_: [pl.BlockSpec((1, tm, 4 * tn_), lambda p, i, j, bi: (bi, i, j))],
        aux=list(spec),
        aux_specs=lambda tm, tn_: [pl.BlockSpec((tm, tn_), lambda p, i, j, bi: (i, order * nj + j))] * 4,
        name="hyena_fwd_" + tag)[0]

    def inv_epi(i, accs, aux_refs, out_refs):
        val = aux_refs[1][0] * (accs[0] + aux_refs[0][0] * aux_refs[2][...])
        for o_ref in out_refs:
            o_ref[0] = val.astype(o_ref.dtype)

    return _dft_call(
        [tabs["inv_c"], tabs["inv_s"]],
        [(yab, lambda p, bi: bi, lambda p, j: 4 * j + 2 * p), (yab, lambda p, bi: bi, lambda p, j: 4 * j + 2 * p + 1)],
        [(0, 0, 0), (0, 1, 1)], 1,
        n_par=2, batch=b, m=half, kdim=half, n=d, tm=1024, tn=tn, epi=inv_epi,
        out_shapes=[jax.ShapeDtypeStruct((2 * b, half, d), dt) for dt in out_dtypes],
        out_specs=lambda tm, tn_: [pl.BlockSpec((1, tm, tn_), lambda p, i, j, bi: (p * b + bi, i, j))
                                   for _ in out_dtypes],
        aux=[src_f32, gate, bias],
        aux_specs=lambda tm, tn_: [
            pl.BlockSpec((1, tm, tn_), lambda p, i, j, bi: (p * b + bi, i, j)),
            pl.BlockSpec((1, tm, tn_), lambda p, i, j, bi: (p * b + bi, i, gate_off * nj + j)),
            pl.BlockSpec((1, tn_), lambda p, i, j, bi: (0, j)),
        ],
        name="hyena_inv_" + tag)


def _hyena_mixer(h, mods, nw_pre, p, g, nw_post):
    b, t, d = h.shape
    sc, sh = mods
    tn = _tile(3 * d, 512)
    half = t // 2
    shapes, specs = _parity_out(b, t, 3 * d, tn, [F32, BF16])
    pj, pj_bf = _proj(h, nw_pre, sc, sh, [(p["w_in"], 0)], n_col_tiles=3 * d // tn, tn=tn, tm=1024,
                      epi=_epi_identity, conv=(p["cw"], p["cb"]), out_shapes=shapes, out_specs=specs,
                      parity=True, name="hyena_in")
    tabs = _hyena_tables(t)
    taps = _hyena_taps(t, p, d)
    spec = _hyena_spectrum(taps, tabs, t, d)
    fbias = p["fbias"]
    pj2 = pj.reshape(2 * b, half, 3 * d)
    z, z_bf = _hyena_long_conv(pj_bf.reshape(2 * b, half, 3 * d), pj2, pj2, 1, spec, 0, fbias[0:1], tabs, d,
                               (F32, BF16), "o1")
    (y,) = _hyena_long_conv(z_bf, z, pj2, 2, spec, 1, fbias[1:2], tabs, d, (BF16,), "o2")
    return _outproj(y.reshape(2, b, half, d), p["w_out"], h, g, nw_post, parity=True, name="hyena_out")


def _conv_ffn(h, sc, sh, g, nw_pre, nw_post, w_up_bf, cw, cb, w_down_bf, tag):
    b, t, d = h.shape
    dff = w_down_bf.shape[0]
    tn = _tile(dff, 512)
    nj = dff // tn
    shapes, specs = _std_out(b, t, dff, tn, BF16)
    act = _proj(h, nw_pre, sc, sh, [(w_up_bf, 0), (w_up_bf, nj)], n_col_tiles=nj, tn=tn, tm=1024,
                epi=_epi_swiglu, conv=(cw, cb.reshape(1, -1)), out_shapes=shapes, out_specs=specs,
                name="ffn_up_" + tag)[0]
    return _outproj(act, w_down_bf, h, g, nw_post, name="ffn_down_" + tag)


def kernel(x, c, ctx, c_ctx, ada_w, ada_b, norm_pre_mix, norm_post_mix, norm_pre_ffn, norm_post_ffn, ffn_up, ffn_conv_w, ffn_conv_b, ffn_down, fn_in, fn_out, attn_in, attn_q_gain, attn_k_gain, attn_out, ssm_in, ssm_conv_w, ssm_conv_b, ssm_dt_bias, ssm_a_log, ssm_d, ssm_norm, ssm_out, hy_in, hy_conv_w, hy_conv_b, hy_f_w1, hy_f_b1, hy_f_w2, hy_f_b2, hy_f_w3, hy_f_freq, hy_decay, hy_f_bias, hy_out):
    b, t, d = x.shape
    tc = ctx.shape[1]
    depth = ada_w.shape[0]
    readers = [i for i in range(depth) if i % N_MIXERS in CTX_READER_KINDS]
    last_reader = max(readers) if readers else -1

    n_rows = -(-(b + 1) // 8) * 8
    c_rows = jnp.zeros((n_rows, d), F32).at[:b].set(c).at[b].set(c_ctx)
    mod = _modulation_all(c_rows, ada_w, ada_b).reshape(depth, n_rows, 6, d)

    h, hc = x, ctx
    for i in range(depth):
        kind, j = i % N_MIXERS, i // N_MIXERS
        ctx_live = i <= last_reader
        ctx_next = i < last_reader
        m_lat = [mod[i, :b, s][:, None, :] for s in range(6)]
        sh1, sc1, g1, sh2, sc2, g2 = m_lat
        if ctx_live:
            m_ctx = [jnp.broadcast_to(mod[i, b, s][None, None, :], (b, 1, d)) for s in range(6)]
            csh1, csc1, cg1, csh2, csc2, cg2 = m_ctx
        nw_pre, nw_post = norm_pre_mix[i], norm_post_mix[i]
        if kind == 0:
            gdim = d // FN_GROUPS
            cc, scs = _fourier_chan_tables(gdim)
            w_fold = _fold_channel_dft(fn_in[j], cc, scs)
            w_out_bf = fn_out[j].astype(BF16)
            h = _fourier_mixer(h, nw_pre, sc1, sh1, w_fold, _fourier_pos_tables(t), w_out_bf, g1, nw_post, "lat")
            if ctx_next:
                hc = _fourier_mixer(hc, nw_pre, csc1, csh1, w_fold, _fourier_pos_tables(tc), w_out_bf, cg1,
                                    nw_post, "ctx")
        elif kind == 1:
            w_in_bf = attn_in[j].astype(BF16)
            w_out_bf = attn_out[j].astype(BF16)
            n_q, n_kv = d // HEAD_DIM, N_KV_HEADS
            gains = jnp.concatenate([jnp.tile(attn_q_gain[j], n_q), jnp.tile(attn_k_gain[j], n_kv)]).reshape(1, -1)
            qkv = _attn_project(h, nw_pre, sc1, sh1, w_in_bf, gains, _rope_tables(t), "lat")
            qkv_c = _attn_project(hc, nw_pre, csc1, csh1, w_in_bf, gains, None, "ctx")
            o = _attention(qkv, qkv_c, d)
            h = _outproj(o, w_out_bf, h, g1, nw_post, name="attn_out_lat")
            if ctx_next:
                oc = _attention(None, qkv_c, d)
                hc = _outproj(oc, w_out_bf, hc, cg1, nw_post, name="attn_out_ctx")
        elif kind == 2:
            w = ssm_in[j]
            d_inner = ssm_out.shape[1]
            n_conv = ssm_conv_w.shape[2]
            p = dict(wz=w[:, :d_inner].astype(BF16), wx=w[:, d_inner:d_inner + n_conv].astype(BF16),
                     wdt=w[:, d_inner + n_conv:].astype(BF16), cw=ssm_conv_w[j], cb=ssm_conv_b[j].reshape(1, -1),
                     dt_bias=ssm_dt_bias[j], a_log=ssm_a_log[j], d_skip=ssm_d[j], norm_w=ssm_norm[j],
                     w_out=ssm_out[j].astype(BF16))
            if ctx_next:
                raise NotImplementedError("context output of the SSD mixer is not needed at this depth")
            h = _ssd_mixer(h, hc, (sc1, sh1), (csc1, csh1), nw_pre, p, g1, nw_post)
        else:
            pad_r = LANES - hy_f_w1.shape[1]
            pad_c = LANES - hy_f_w1.shape[2]
            p = dict(w_in=hy_in[j].astype(BF16), cw=hy_conv_w[j], cb=hy_conv_b[j].reshape(1, -1),
                     fw1=jnp.pad(hy_f_w1[j], ((0, pad_r), (0, pad_c))),
                     fb1=jnp.pad(hy_f_b1[j], (0, pad_c)).reshape(1, LANES),
                     fw2=jnp.pad(hy_f_w2[j], ((0, pad_c), (0, pad_c))),
                     fb2=jnp.pad(hy_f_b2[j], (0, pad_c)).reshape(1, LANES),
                     ffreq=jnp.pad(hy_f_freq[j], (0, pad_c)).reshape(1, LANES),
                     fw3=jnp.pad(hy_f_w3[j], ((0, pad_c), (0, 0))), decay=hy_decay[j].reshape(1, -1),
                     fbias=hy_f_bias[j], w_out=hy_out[j].astype(BF16))
            if ctx_next:
                raise NotImplementedError("context output of the Hyena mixer is not needed at this depth")
            h = _hyena_mixer(h, (sc1, sh1), nw_pre, p, g1, nw_post)
        w_up_bf = ffn_up[i].astype(BF16)
        w_down_bf = ffn_down[i].astype(BF16)
        h = _conv_ffn(h, sc2, sh2, g2, norm_pre_ffn[i], norm_post_ffn[i], w_up_bf, ffn_conv_w[i], ffn_conv_b[i],
                      w_down_bf, "lat")
        if ctx_next:
            hc = _conv_ffn(hc, csc2, csh2, cg2, norm_pre_ffn[i], norm_post_ffn[i], w_up_bf, ffn_conv_w[i],
                           ffn_conv_b[i], w_down_bf, "ctx")
    return h
```

```python
import functools
import math

import jax
import jax.numpy as jnp
from jax import lax
from jax.experimental import pallas as pl
from jax.experimental.pallas import tpu as pltpu

F32 = jnp.float32
BF16 = jnp.bfloat16

GRID_W = 64
N_MIXERS = 4
CTX_READER_KINDS = (1, 2)
RMS_EPS = 1e-6
FN_GROUPS = 4
HEAD_DIM = 128
N_KV_HEADS = 4
ROPE_THETA = 10000.0
ROPE_FREQS = HEAD_DIM // 4
ATTN_SCALE = HEAD_DIM ** -0.5
SSM_HEAD_DIM = 64
SSM_GROUPS = 8
SSM_STATE = 128
SSM_CHUNK = 128
SSD_GROUPS_PER_STEP = 8
HY_ORDER = 2
HY_EMB = 33
HY_BANDS = (HY_EMB - 1) // 2

V7X_VMEM_LIMIT_BYTES = 56 * 1024 * 1024
LANES = 128
CONV_HALO = 16
PROJ_ROW_CHUNK = 128
NEG_BIG = -1e30


def _cparams(*sem):
    return pltpu.CompilerParams(dimension_semantics=sem, vmem_limit_bytes=V7X_VMEM_LIMIT_BYTES)


def _tile(n, pref, mult=LANES):
    if n <= pref:
        return n
    t = (pref // mult) * mult
    while t >= mult:
        if n % t == 0:
            return t
        t -= mult
    return n


def _silu(x):
    return x * (1.0 / (1.0 + jnp.exp(-x)))


def _split_bf16(v):
    hi = v.astype(BF16)
    lo = (v - hi.astype(F32)).astype(BF16)
    return hi, lo


def _dot(a, b):
    return jnp.dot(a, b, preferred_element_type=F32)


def _dot_nt(a, b):
    return lax.dot_general(a, b, (((1,), (1,)), ((), ())), preferred_element_type=F32)


def _dot3(a, b):
    ah, al = _split_bf16(a)
    bh, bl = _split_bf16(b)
    return _dot(ah, bh) + _dot(ah, bl) + _dot(al, bh)


def _rms_rows(x, w):
    return x * lax.rsqrt(jnp.mean(x * x, axis=-1, keepdims=True) + RMS_EPS) * w


def _mod_kernel(c_ref, w_ref, b_ref, o_ref):
    a = _silu(c_ref[...]).astype(BF16)
    o_ref[0] = _dot(a, w_ref[0].astype(BF16)) + b_ref[0]


def _modulation_all(c_rows, ada_w, ada_b):
    depth, d, n6 = ada_w.shape
    r = c_rows.shape[0]
    tn = _tile(n6, 1024)
    return pl.pallas_call(
        _mod_kernel,
        grid=(depth, n6 // tn),
        in_specs=[
            pl.BlockSpec((r, d), lambda l, j: (0, 0)),
            pl.BlockSpec((1, d, tn), lambda l, j: (l, 0, j)),
            pl.BlockSpec((1, 1, tn), lambda l, j: (l, 0, j)),
        ],
        out_specs=pl.BlockSpec((1, r, tn), lambda l, j: (l, 0, j)),
        out_shape=jax.ShapeDtypeStruct((depth, r, n6), F32),
        compiler_params=_cparams("parallel", "parallel"),
        name="ada_mod",
    )(c_rows, ada_w, ada_b.reshape(depth, 1, n6))


def _proj_prologue(h_ref, halo_refs, nw_ref, sc_ref, sh_ref, u_ref, i, tm, n_row_tiles):
    hl = CONV_HALO
    off = hl if halo_refs else 0
    rc = _tile(tm, PROJ_ROW_CHUNK, 16)
    wv = nw_ref[...] * (1.0 + sc_ref[0])
    shv = sh_ref[0]

    def nm(x):
        return x * lax.rsqrt(jnp.mean(x * x, axis=-1, keepdims=True) + RMS_EPS) * wv + shv

    x = h_ref[0]
    inv = lax.rsqrt(jnp.mean(x * x, axis=-1, keepdims=True) + RMS_EPS)
    for r0 in range(0, tm, rc):
        u_ref[off + r0:off + r0 + rc, :] = (h_ref[0, r0:r0 + rc, :] * inv[r0:r0 + rc] * wv + shv).astype(BF16)
    if halo_refs:
        hp_ref, hn_ref = halo_refs
        u_ref[0:hl, :] = jnp.where(i == 0, 0.0, nm(hp_ref[0])).astype(BF16)
        u_ref[hl + tm:, :] = jnp.where(i == n_row_tiles - 1, 0.0, nm(hn_ref[0])).astype(BF16)


def _proj_kernel(*refs, n_w, use_conv, n_aux, n_out, epi, tm, n_row_tiles, parity):
    refs = list(refs)
    h_ref = refs.pop(0)
    halo_refs = (refs.pop(0), refs.pop(0)) if use_conv else ()
    nw_ref, sc_ref, sh_ref = refs.pop(0), refs.pop(0), refs.pop(0)
    w_refs = [refs.pop(0) for _ in range(n_w)]
    cw_refs, cb_refs = [], []
    if use_conv:
        for _ in range(n_w):
            cw_refs.append(refs.pop(0))
            cb_refs.append(refs.pop(0))
    aux_refs = [refs.pop(0) for _ in range(n_aux)]
    out_refs = [refs.pop(0) for _ in range(n_out)]
    u_ref = refs.pop(0)
    p_ref = refs.pop(0) if parity else None
    i = pl.program_id(1)
    j = pl.program_id(2)
    hl = CONV_HALO
    off = hl if use_conv else 0

    @pl.when(j == 0)
    def _():
        _proj_prologue(h_ref, halo_refs, nw_ref, sc_ref, sh_ref, u_ref, i, tm, n_row_tiles)

    tn = w_refs[0].shape[1]
    cols = slice(0, tn)
    if parity:
        hm = tm // 2
        for k in range(n_w):
            p = _dot(u_ref[...], w_refs[k][...])
            for sl in range(tn // LANES):
                p_ref[k, sl] = p[:, sl * LANES:(sl + 1) * LANES]
        for par in (0, 1):
            ys = []
            for k in range(n_w):
                slabs = []
                for sl in range(tn // LANES):
                    lanes = slice(sl * LANES, (sl + 1) * LANES)
                    rows = lambda shift: p_ref[k, sl, pl.ds(off + par + shift, hm, stride=2), :]
                    if use_conv:
                        cw = cw_refs[k][:, lanes]
                        slabs.append(cw[0:1] * rows(-1) + cw[1:2] * rows(0) + cw[2:3] * rows(1)
                                     + cb_refs[k][:, lanes])
                    else:
                        slabs.append(rows(0))
                ys.append(jnp.concatenate(slabs, axis=1))
            epi(j, ys, aux_refs, [o.at[par] for o in out_refs], slice(0, hm), cols)
        return
    ys = []
    for k in range(n_w):
        p = _dot(u_ref[...], w_refs[k][...])
        if use_conv:
            cw = cw_refs[k][...]
            prev = pltpu.roll(p, 1, axis=0)
            nxt = pltpu.roll(p, tm + 2 * hl - 1, axis=0)
            p = (cw[0:1] * prev + cw[1:2] * p + cw[2:3] * nxt + cb_refs[k][...])[hl:hl + tm]
        ys.append(p)
    epi(j, ys, aux_refs, out_refs, slice(0, tm), cols)


def _proj_piped_kernel(h_ref, nw_ref, sc_ref, sh_ref, w_ref, *refs, n_aux, epi, tm, n_row_tiles, n_col_tiles):
    aux_refs = list(refs[:n_aux])
    o_ref, u_ref, pa_ref, pb_ref = refs[n_aux:]
    s = pl.program_id(1)
    n_pairs = n_row_tiles * n_col_tiles
    cur = jnp.minimum(s, n_pairs - 1)
    j_prev = jnp.maximum(s - 1, 0) % n_col_tiles
    tn = w_ref.shape[1]

    @pl.when((cur % n_col_tiles == 0) & (s < n_pairs))
    def _():
        _proj_prologue(h_ref, (), nw_ref, sc_ref, sh_ref, u_ref, cur // n_col_tiles, tm, n_row_tiles)

    @pl.when(s == 0)
    def _():
        pb_ref[...] = jnp.zeros(pb_ref.shape, F32)

    def body(store_ref, load_ref):
        for sl in range(tn // LANES):
            epi(j_prev, [load_ref[sl]], aux_refs, [o_ref], slice(0, tm), slice(sl * LANES, (sl + 1) * LANES))
        p = _dot(u_ref[...], w_ref[...])
        for sl in range(tn // LANES):
            store_ref[sl] = p[:, sl * LANES:(sl + 1) * LANES]

    @pl.when(s % 2 == 0)
    def _():
        body(pa_ref, pb_ref)

    @pl.when(s % 2 == 1)
    def _():
        body(pb_ref, pa_ref)


def _proj(h, nw, sc, sh, w_views, *, n_col_tiles, tn, tm, epi, out_shapes, out_specs,
          conv=None, aux=(), aux_specs=(), parity=False, name):
    b, t, d = h.shape
    tm = _tile(t, tm, 16)
    n_row_tiles = t // tm
    use_conv = conv is not None
    hl = CONV_HALO
    ins = [h]
    specs = [pl.BlockSpec((1, tm, d), lambda bi, i, j: (bi, i, 0))]
    if use_conv:
        r = tm // hl
        last = t // hl - 1
        ins += [h, h]
        specs += [
            pl.BlockSpec((1, hl, d), lambda bi, i, j: (bi, jnp.maximum(i * r - 1, 0), 0)),
            pl.BlockSpec((1, hl, d), lambda bi, i, j: (bi, jnp.minimum((i + 1) * r, last), 0)),
        ]
    ins += [nw.reshape(1, d), sc, sh]
    specs += [
        pl.BlockSpec((1, d), lambda bi, i, j: (0, 0)),
        pl.BlockSpec((1, 1, d), lambda bi, i, j: (bi, 0, 0)),
        pl.BlockSpec((1, 1, d), lambda bi, i, j: (bi, 0, 0)),
    ]
    for w, off in w_views:
        ins.append(w)
        specs.append(pl.BlockSpec((d, tn), lambda bi, i, j, off=off: (0, off + j)))
    if use_conv:
        cw, cb = conv
        for _, off in w_views:
            ins += [cw, cb]
            specs += [
                pl.BlockSpec((cw.shape[0], tn), lambda bi, i, j, off=off: (0, off + j)),
                pl.BlockSpec((1, tn), lambda bi, i, j, off=off: (0, off + j)),
            ]
    ins += list(aux)
    specs += list(aux_specs)
    rows = tm + 2 * hl if use_conv else tm
    scratch = [pltpu.VMEM((rows, d), BF16)]
    if parity:
        scratch.append(pltpu.VMEM((len(w_views), tn // LANES, rows, LANES), F32))
    kern = functools.partial(_proj_kernel, n_w=len(w_views), use_conv=use_conv, n_aux=len(aux),
                             n_out=len(out_shapes), epi=epi, tm=tm, n_row_tiles=n_row_tiles, parity=parity)
    return pl.pallas_call(
        kern,
        grid=(b, n_row_tiles, n_col_tiles),
        in_specs=specs,
        out_specs=out_specs(tm),
        out_shape=out_shapes,
        scratch_shapes=scratch,
        compiler_params=_cparams("parallel", "parallel", "arbitrary"),
        name=name,
    )(*ins)


def _proj_piped(h, nw, sc, sh, w, *, n_col_tiles, tn, tm, epi, out_shape, out_spec, aux, aux_specs, name):
    b, t, d = h.shape
    n_row_tiles = t // tm
    n_pairs = n_row_tiles * n_col_tiles
    cur = lambda s: jnp.minimum(s, n_pairs - 1)
    prv = lambda s: jnp.maximum(s - 1, 0)

    def finished(spec):
        return pl.BlockSpec(spec.block_shape,
                            lambda bi, s, f=spec.index_map: f(bi, prv(s) // n_col_tiles, prv(s) % n_col_tiles))

    specs = [
        pl.BlockSpec((1, tm, d), lambda bi, s: (bi, cur(s) // n_col_tiles, 0)),
        pl.BlockSpec((1, d), lambda bi, s: (0, 0)),
        pl.BlockSpec((1, 1, d), lambda bi, s: (bi, 0, 0)),
        pl.BlockSpec((1, 1, d), lambda bi, s: (bi, 0, 0)),
        pl.BlockSpec((d, tn), lambda bi, s: (0, cur(s) % n_col_tiles)),
    ] + [finished(sp) for sp in aux_specs]
    park = pltpu.VMEM((tn // LANES, tm, LANES), F32)
    return pl.pallas_call(
        functools.partial(_proj_piped_kernel, n_aux=len(aux), epi=epi, tm=tm, n_row_tiles=n_row_tiles,
                          n_col_tiles=n_col_tiles),
        grid=(b, n_pairs + 1),
        in_specs=specs,
        out_specs=finished(out_spec),
        out_shape=out_shape,
        scratch_shapes=[pltpu.VMEM((tm, d), BF16), park, park],
        compiler_params=_cparams("parallel", "arbitrary"),
        name=name,
    )(h, nw.reshape(1, d), sc, sh, w, *aux)


def _std_out(b, t, n, tn, dtype):
    shapes = [jax.ShapeDtypeStruct((b, t, n), dtype)]
    specs = lambda tm: [pl.BlockSpec((1, tm, tn), lambda bi, i, j: (bi, i, j))]
    return shapes, specs


def _parity_out(b, t, n, tn, dtypes):
    shapes = [jax.ShapeDtypeStruct((2, b, t // 2, n), dt) for dt in dtypes]
    specs = lambda tm: [pl.BlockSpec((2, 1, tm // 2, tn), lambda bi, i, j: (0, bi, i, j)) for _ in dtypes]
    return shapes, specs


def _epi_identity(j, ys, aux_refs, out_refs, rows, cols):
    for o_ref in out_refs:
        o_ref[0, rows, cols] = ys[0].astype(o_ref.dtype)


def _epi_silu(j, ys, aux_refs, out_refs, rows, cols):
    out_refs[0][0, rows, cols] = _silu(ys[0]).astype(out_refs[0].dtype)


def _epi_swiglu(j, ys, aux_refs, out_refs, rows, cols):
    out_refs[0][0, rows, cols] = (_silu(ys[0]) * ys[1]).astype(out_refs[0].dtype)


def _outproj_kernel(a_ref, w_ref, h_ref, g_ref, nw_ref, o_ref, acc_ref, *, nj, tn):
    j = pl.program_id(2)
    acc_ref[j] = _dot(a_ref[0], w_ref[...])

    @pl.when(j == nj - 1)
    def _():
        ss = None
        for jj in range(nj):
            o = acc_ref[jj]
            s = jnp.sum(o * o, axis=-1, keepdims=True)
            ss = s if ss is None else ss + s
        inv = lax.rsqrt(ss * (1.0 / (nj * tn)) + RMS_EPS)
        for jj in range(nj):
            cols = slice(jj * tn, (jj + 1) * tn)
            o_ref[0, :, cols] = h_ref[0, :, cols] + g_ref[0, :, cols] * (acc_ref[jj] * inv * nw_ref[:, cols])


def _outproj_parity_kernel(a_ref, w_ref, h_ref, g_ref, nw_ref, o_ref, acc_ref, *, nj, tn, hm):
    j = pl.program_id(2)
    spt = tn // LANES
    for par in (0, 1):
        r = _dot(a_ref[par, 0], w_ref[...])
        for sl in range(spt):
            acc_ref[j * spt + sl, pl.ds(par, hm, stride=2), :] = r[:, sl * LANES:(sl + 1) * LANES]

    @pl.when(j == nj - 1)
    def _():
        ss = None
        for sl in range(nj * spt):
            o = acc_ref[sl]
            s = jnp.sum(o * o, axis=-1, keepdims=True)
            ss = s if ss is None else ss + s
        inv = lax.rsqrt(ss * (1.0 / (nj * tn)) + RMS_EPS)
        for sl in range(nj * spt):
            cols = slice(sl * LANES, (sl + 1) * LANES)
            o_ref[0, :, cols] = h_ref[0, :, cols] + g_ref[0, :, cols] * (acc_ref[sl] * inv * nw_ref[:, cols])


def _outproj(a, w, h, g, nw, *, tm=512, tn=512, parity=False, name):
    b, t, d = h.shape
    kdim = w.shape[0]
    tm = _tile(t, tm, 16)
    tn = _tile(d, tn)
    nj = d // tn
    if parity:
        hm = tm // 2
        return pl.pallas_call(
            functools.partial(_outproj_parity_kernel, nj=nj, tn=tn, hm=hm),
            grid=(b, t // tm, nj),
            in_specs=[
                pl.BlockSpec((2, 1, hm, kdim), lambda bi, i, j: (0, bi, i, 0)),
                pl.BlockSpec((kdim, tn), lambda bi, i, j: (0, j)),
                pl.BlockSpec((1, tm, d), lambda bi, i, j: (bi, i, 0)),
                pl.BlockSpec((1, 1, d), lambda bi, i, j: (bi, 0, 0)),
                pl.BlockSpec((1, d), lambda bi, i, j: (0, 0)),
            ],
            out_specs=pl.BlockSpec((1, tm, d), lambda bi, i, j: (bi, i, 0)),
            out_shape=jax.ShapeDtypeStruct((b, t, d), F32),
            scratch_shapes=[pltpu.VMEM((d // LANES, tm, LANES), F32)],
            compiler_params=_cparams("parallel", "parallel", "arbitrary"),
            name=name,
        )(a, w, h, g, nw.reshape(1, d))
    return pl.pallas_call(
        functools.partial(_outproj_kernel, nj=nj, tn=tn),
        grid=(b, t // tm, nj),
        in_specs=[
            pl.BlockSpec((1, tm, kdim), lambda bi, i, j: (bi, i, 0)),
            pl.BlockSpec((kdim, tn), lambda bi, i, j: (0, j)),
            pl.BlockSpec((1, tm, d), lambda bi, i, j: (bi, i, 0)),
            pl.BlockSpec((1, 1, d), lambda bi, i, j: (bi, 0, 0)),
            pl.BlockSpec((1, d), lambda bi, i, j: (0, 0)),
        ],
        out_specs=pl.BlockSpec((1, tm, d), lambda bi, i, j: (bi, i, 0)),
        out_shape=jax.ShapeDtypeStruct((b, t, d), F32),
        scratch_shapes=[pltpu.VMEM((nj, tm, tn), F32)],
        compiler_params=_cparams("parallel", "parallel", "arbitrary"),
        name=name,
    )(a, w, h, g, nw.reshape(1, d))


def _dft_kernel(*refs, n_l, n_r, n_acc, terms, n_aux, n_out, epi):
    refs = list(refs)
    l_refs = [refs.pop(0) for _ in range(n_l)]
    r_refs = [refs.pop(0) for _ in range(n_r)]
    aux_refs = [refs.pop(0) for _ in range(n_aux)]
    out_refs = [refs.pop(0) for _ in range(n_out)]
    accs = [None] * n_acc
    for o, l, r in terms:
        dd = _dot(l_refs[l][0], r_refs[r][0])
        accs[o] = dd if accs[o] is None else accs[o] + dd
    epi(pl.program_id(1), accs, aux_refs, out_refs)


def _dft_call(l_tabs, r_views, terms, n_acc, *, n_par, batch, m, kdim, n, tm, tn, epi, out_shapes, out_specs,
              aux=(), aux_specs=None, name):
    tm = _tile(m, tm, 16)
    tn = _tile(n, tn)
    ins, specs = [], []
    for lt in l_tabs:
        ins.append(lt)
        specs.append(pl.BlockSpec((1, tm, kdim), lambda p, i, j, b: (p, i, 0)))
    for arr, lead_fn, col_fn in r_views:
        ins.append(arr)
        specs.append(pl.BlockSpec((1, kdim, tn),
                                  lambda p, i, j, b, lf=lead_fn, cf=col_fn: (lf(p, b), 0, cf(p, j))))
    ins += list(aux)
    if aux_specs is not None:
        specs += list(aux_specs(tm, tn))
    kern = functools.partial(_dft_kernel, n_l=len(l_tabs), n_r=len(r_views), n_acc=n_acc, terms=terms,
                             n_aux=len(aux), n_out=len(out_shapes), epi=epi)
    return pl.pallas_call(
        kern,
        grid=(n_par, m // tm, n // tn, batch),
        in_specs=specs,
        out_specs=out_specs(tm, tn),
        out_shape=out_shapes,
        compiler_params=_cparams("parallel", "parallel", "parallel", "parallel"),
        name=name,
    )(*ins)


def _int_grid(n):
    k = lax.broadcasted_iota(jnp.int32, (n, n), 0)
    j = lax.broadcasted_iota(jnp.int32, (n, n), 1)
    return k, j


def _fourier_pos_tables(n):
    k, j = _int_grid(n // 2)
    tabs = []
    for par in (0, 1):
        ang = ((k * (2 * j + par)) % n).astype(F32) * (2.0 * math.pi / n)
        tabs += [jnp.cos(ang).astype(BF16)[None], (-jnp.sin(ang)).astype(BF16)[None]]
    return tabs


def _fourier_chan_tables(gdim):
    kc, jc = _int_grid(gdim)
    angc = ((kc * jc) % gdim).astype(F32) * (2.0 * math.pi / gdim)
    return jnp.cos(angc), jnp.sin(angc)


def _wfold_kernel(w_ref, c_ref, s_ref, oc_ref, os_ref):
    w = w_ref[...]
    oc_ref[...] = _dot3(w, c_ref[...]).astype(BF16)
    os_ref[...] = _dot3(w, s_ref[...]).astype(BF16)


def _fold_channel_dft(w_in, cc, sc):
    d = w_in.shape[0]
    gdim = cc.shape[0]
    ng = d // gdim
    oc, os_ = pl.pallas_call(
        _wfold_kernel,
        grid=(ng,),
        in_specs=[
            pl.BlockSpec((d, gdim), lambda g: (0, g)),
            pl.BlockSpec((gdim, gdim), lambda g: (0, 0)),
            pl.BlockSpec((gdim, gdim), lambda g: (0, 0)),
        ],
        out_specs=[pl.BlockSpec((d, gdim), lambda g: (0, g)), pl.BlockSpec((d, gdim), lambda g: (0, g))],
        out_shape=[jax.ShapeDtypeStruct((d, d), BF16), jax.ShapeDtypeStruct((d, d), BF16)],
        compiler_params=_cparams("parallel"),
        name="fourier_fold",
    )(w_in, cc, sc)
    return jnp.concatenate([oc, os_], axis=1)


def _fourier_mixer(h, nw, sc, sh, w_fold, tables, w_out_bf, g, nw_post, tag):
    b, t, d = h.shape
    half = t // 2
    tn = _tile(2 * d, 512)
    shapes, specs = _parity_out(b, t, 2 * d, tn, [BF16])
    a = _proj(h, nw, sc, sh, [(w_fold, 0)], n_col_tiles=2 * d // tn, tn=tn, tm=1024, epi=_epi_identity,
              out_shapes=shapes, out_specs=specs, parity=True, name="fourier_in_" + tag)[0]
    scale = 1.0 / math.sqrt(t * (d // FN_GROUPS))

    def epi(i, accs, aux_refs, out_refs):
        ev, od = accs
        out_refs[0][0, 0] = ((ev + od) * scale).astype(BF16)
        out_refs[0][0, 1] = ((ev - od) * scale).astype(BF16)

    a2 = a.reshape(2 * b, half, 2 * d)
    tn2 = _tile(d, 512)
    nj = d // tn2
    views = [(a2, lambda p, bi, par=par: par * b + bi, lambda p, j, q=q: q * nj + j)
             for par in (0, 1) for q in (0, 1)]
    f = _dft_call(
        tables, views, [(0, 0, 0), (0, 1, 1), (1, 2, 2), (1, 3, 3)], 2,
        n_par=1, batch=b, m=half, kdim=half, n=d, tm=512, tn=tn2, epi=epi,
        out_shapes=[jax.ShapeDtypeStruct((b, 2, half, d), BF16)],
        out_specs=lambda tm, tn_: [pl.BlockSpec((1, 2, tm, tn_), lambda p, i, j, bi: (bi, 0, i, j))],
        name="fourier_pos_" + tag)[0]
    return _outproj(f.reshape(b, t, d), w_out_bf, h, g, nw_post, name="fourier_out_" + tag)


def _rope_tables(n_tokens):
    rows = n_tokens // GRID_W
    row = jnp.broadcast_to(jnp.arange(rows, dtype=F32)[:, None], (rows, GRID_W)).reshape(n_tokens)
    col = jnp.broadcast_to(jnp.arange(GRID_W, dtype=F32)[None, :], (rows, GRID_W)).reshape(n_tokens)
    inv_freq = ROPE_THETA ** (-jnp.arange(ROPE_FREQS, dtype=F32) / ROPE_FREQS)
    ar, ac = row[:, None] * inv_freq, col[:, None] * inv_freq
    cos = jnp.concatenate([jnp.cos(ar), jnp.cos(ar), jnp.cos(ac), jnp.cos(ac)], axis=1)
    sin = jnp.concatenate([-jnp.sin(ar), jnp.sin(ar), -jnp.sin(ac), jnp.sin(ac)], axis=1)
    return cos, sin


def _attn_in_epi(j, ys, aux_refs, out_refs, rows, cols, *, n_q_tiles, rope):
    y = ys[0]
    o_ref = out_refs[0]
    gain = aux_refs[0][:, cols] * jnp.where(j < n_q_tiles, ATTN_SCALE, 1.0)
    t = y * lax.rsqrt(jnp.mean(y * y, axis=-1, keepdims=True) + RMS_EPS) * gain
    if rope:
        cos, sin = aux_refs[1][rows, :], aux_refs[2][rows, :]
        lane = lax.broadcasted_iota(jnp.int32, t.shape, 1)
        half = ROPE_FREQS
        partner = jnp.where(lane % (2 * half) < half,
                            pltpu.roll(t, HEAD_DIM - half, axis=1), pltpu.roll(t, half, axis=1))
        t = t * cos + partner * sin
    o_ref[0, rows, cols] = jnp.where(j <= n_q_tiles, t, y).astype(o_ref.dtype)


def _attn_project(h, nw, sc, sh, w_in_bf, gains, rope_tabs, tag):
    b, t, d = h.shape
    n = w_in_bf.shape[1]
    tn = N_KV_HEADS * HEAD_DIM
    n_q_tiles = d // tn
    rope = rope_tabs is not None
    aux = [gains]
    aux_specs = [pl.BlockSpec((1, tn), lambda bi, i, j: (0, jnp.minimum(j, n_q_tiles)))]
    tm = _tile(t, 1024, 16)
    if rope:
        aux += list(rope_tabs)
        aux_specs += [pl.BlockSpec((tm, HEAD_DIM), lambda bi, i, j: (i, 0))] * 2
    epi = functools.partial(_attn_in_epi, n_q_tiles=n_q_tiles, rope=rope)
    return _proj_piped(h, nw, sc, sh, w_in_bf, n_col_tiles=n // tn, tn=tn, tm=tm, epi=epi,
                       out_shape=jax.ShapeDtypeStruct((b, t, n), BF16),
                       out_spec=pl.BlockSpec((1, tm, tn), lambda bi, i, j: (bi, i, j)),
                       aux=aux, aux_specs=aux_specs, name="attn_in_" + tag)


def _attn_kernel(*refs, has_lat, tkv, n_lat_chunks):
    if has_lat:
        q_ref, kc_ref, vc_ref, k_ref, v_ref, o_ref = refs
    else:
        q_ref, kc_ref, vc_ref, o_ref = refs
    q = q_ref[0]
    n_chunks = 1 + (n_lat_chunks if has_lat else 0)

    def keys(c):
        return kc_ref[0] if c == 0 else k_ref[0, (c - 1) * tkv:c * tkv, :]

    def values(c):
        return vc_ref[0] if c == 0 else v_ref[0, (c - 1) * tkv:c * tkv, :]

    s = _dot_nt(q, keys(0))
    m = l = acc = None
    for c in range(n_chunks):
        s_max = jnp.max(s, axis=-1, keepdims=True)
        m_new = s_max if c == 0 else jnp.maximum(m, s_max)
        p = jnp.exp(s - m_new)
        p_sum = jnp.sum(p, axis=-1, keepdims=True)
        if c + 1 < n_chunks:
            s = _dot_nt(q, keys(c + 1))
        pv = _dot(p.astype(BF16), values(c))
        if c == 0:
            l, acc = p_sum, pv
        else:
            alpha = jnp.exp(m - m_new)
            l = alpha * l + p_sum
            acc = alpha * acc + pv
        m = m_new
    o_ref[0] = (acc / l).astype(o_ref.dtype)


def _attention(qkv, qkv_ctx, d, *, tq=512, tkv=1024):
    has_lat = qkv is not None
    src = qkv if has_lat else qkv_ctx
    b, t, _ = src.shape
    tc = qkv_ctx.shape[1]
    n_heads = d // HEAD_DIM
    grp = n_heads // N_KV_HEADS
    k_off = n_heads
    v_off = n_heads + N_KV_HEADS
    tq = _tile(t, tq, 16)
    tkv = _tile(t, tkv)
    ins = [src, qkv_ctx, qkv_ctx]
    specs = [
        pl.BlockSpec((1, tq, HEAD_DIM), lambda bi, hh, i: (bi, i, hh)),
        pl.BlockSpec((1, tc, HEAD_DIM), lambda bi, hh, i: (bi, 0, k_off + hh // grp)),
        pl.BlockSpec((1, tc, HEAD_DIM), lambda bi, hh, i: (bi, 0, v_off + hh // grp)),
    ]
    if has_lat:
        ins += [qkv, qkv]
        specs += [
            pl.BlockSpec((1, t, HEAD_DIM), lambda bi, hh, i: (bi, 0, k_off + hh // grp)),
            pl.BlockSpec((1, t, HEAD_DIM), lambda bi, hh, i: (bi, 0, v_off + hh // grp)),
        ]
    return pl.pallas_call(
        functools.partial(_attn_kernel, has_lat=has_lat, tkv=tkv, n_lat_chunks=t // tkv),
        grid=(b, n_heads, t // tq),
        in_specs=specs,
        out_specs=pl.BlockSpec((1, tq, HEAD_DIM), lambda bi, hh, i: (bi, i, hh)),
        out_shape=jax.ShapeDtypeStruct((b, t, d), BF16),
        compiler_params=_cparams("parallel", "parallel", "parallel"),
        name="attn_core_lat" if has_lat else "attn_core_ctx",
    )(*ins)


def _dt_epi(j, ys, aux_refs, out_refs, rows, cols):
    x = ys[0] + aux_refs[0][:, cols]
    dt = jnp.maximum(x, 0.0) + jnp.log(1.0 + jnp.exp(-jnp.abs(x)))
    out_refs[0][0, rows, cols] = dt
    out_refs[1][0, rows, cols] = dt * (-jnp.exp(aux_refs[1][:, cols]))


def _ssd_kernel(*refs, rev, with_output, final, nc):
    refs = list(refs)
    x_ref, b_ref, c_ref, dt_ref, adt_ref, s0_ref = [refs.pop(0) for _ in range(6)]
    if final:
        yprev_ref, z_ref, dskip_ref, nw_ref = [refs.pop(0) for _ in range(4)]
    y_ref = refs.pop(0) if with_output else None
    sfin_ref = refs.pop(0)
    st_ref = refs.pop(0)
    c = pl.program_id(2)
    q = SSM_CHUNK
    hp = SSM_HEAD_DIM
    ns = SSM_STATE
    gpb = dt_ref.shape[1]
    n_hg = dt_ref.shape[2]
    width = n_hg * hp

    @pl.when(c == 0)
    def _():
        st_ref[...] = s0_ref[0]

    ii = lax.broadcasted_iota(jnp.int32, (q, q), 0)
    jj = lax.broadcasted_iota(jnp.int32, (q, q), 1)
    mask = (jj >= ii) if rev else (jj <= ii)
    mask_bf = mask.astype(F32).astype(BF16)
    mask_t_bf = ((ii >= jj) if rev else (ii <= jj)).astype(F32).astype(BF16)
    eye_bf = (ii == jj).astype(F32).astype(BF16)
    lane = lax.broadcasted_iota(jnp.int32, (q, 2 * hp), 1)

    def hi_lo(v):
        hi = v.astype(BF16).astype(F32)
        return hi, v - hi

    def expand_rows(v):
        return jnp.concatenate([jnp.broadcast_to(v[hh:hh + 1, :], (hp, q)) for hh in range(n_hg)], axis=0)

    groups = range(gpb)
    cols = [slice(gi * width, (gi + 1) * width) for gi in groups]
    scols = [slice(gi * ns, (gi + 1) * ns) for gi in groups]
    dt_r = [dt_ref[0, gi] for gi in groups]
    a_parts = [hi_lo(adt_ref[0, gi]) for gi in groups]
    d_parts = [hi_lo(dt_r[gi]) for gi in groups]
    e_acs = [_dot_nt(mask_bf, expand_rows(a_parts[gi][0]).astype(BF16))
             + _dot_nt(mask_bf, expand_rows(a_parts[gi][1]).astype(BF16)) for gi in groups]
    e_dt = [_dot_nt(eye_bf, expand_rows(d_parts[gi][0]).astype(BF16))
            + _dot_nt(eye_bf, expand_rows(d_parts[gi][1]).astype(BF16)) for gi in groups]
    last = [e_acs[gi][0:1, :] if rev else e_acs[gi][q - 1:q, :] for gi in groups]

    x = [x_ref[0, :, cols[gi]] for gi in groups]
    bm = [b_ref[0, :, scols[gi]] for gi in groups]
    s_old = [st_ref[gi] for gi in groups]
    for gi in groups:
        wend = jnp.exp(last[gi] - e_acs[gi]) * e_dt[gi]
        s_new = s_old[gi] * jnp.exp(last[gi]) + _dot(bm[gi].T.astype(BF16), (x[gi] * wend).astype(BF16))
        st_ref[gi] = s_new

        @pl.when(c == nc - 1)
        def _():
            sfin_ref[0, gi] = s_new

    if not with_output:
        return
    cm_bf = [c_ref[0, :, scols[gi]].astype(BF16) for gi in groups]
    cb = [_dot_nt(cm_bf[gi], bm[gi].astype(BF16)) for gi in groups]
    acs_r = []
    for gi in groups:
        r16 = _dot(jnp.concatenate(a_parts[gi], axis=0).astype(BF16), mask_t_bf)
        acs_r.append(r16[0:n_hg] + r16[n_hg:2 * n_hg])
    y_state = [_dot(cm_bf[gi], s_old[gi].astype(BF16)) * jnp.exp(e_acs[gi]) for gi in groups]
    ys = [[] for _ in groups]
    for pair in range(n_hg // 2):
        for gi in groups:
            ws = []
            for hh in (2 * pair, 2 * pair + 1):
                seg = e_acs[gi][:, hh * hp:hh * hp + 1] - acs_r[gi][hh:hh + 1, :]
                decay = jnp.exp(jnp.where(mask, seg, NEG_BIG))
                ws.append((cb[gi] * decay * dt_r[gi][hh:hh + 1, :]).astype(BF16))
            xp = x[gi][:, pair * 2 * hp:(pair + 1) * 2 * hp]
            bd = jnp.concatenate([jnp.where(lane < hp, xp, 0.0), jnp.where(lane >= hp, xp, 0.0)], axis=0)
            ys[gi].append(_dot(jnp.concatenate(ws, axis=1), bd.astype(BF16)))
    for gi in groups:
        y = jnp.concatenate(ys[gi], axis=1) + y_state[gi]
        if final:
            yt = yprev_ref[0, :, cols[gi]] + y + x[gi] * dskip_ref[:, cols[gi]]
            yt = yt * _silu(z_ref[0, :, cols[gi]])
            y_ref[0, :, cols[gi]] = _rms_rows(yt, nw_ref[:, cols[gi]]).astype(y_ref.dtype)
        else:
            y_ref[0, :, cols[gi]] = y


def _ssd_scan(xbc, dt_rows, adt_rows, s0, d_inner, *, rev, with_output, final_args=None):
    b, t, _ = xbc.shape
    g = SSM_GROUPS
    q = SSM_CHUNK
    nc = t // q
    width = d_inner // g
    n_hg = width // SSM_HEAD_DIM
    gpb = SSD_GROUPS_PER_STEP
    ngb = g // gpb
    wblk = gpb * width
    sblk = gpb * SSM_STATE
    b_off = d_inner // sblk
    c_off = b_off + ngb
    final = final_args is not None
    cmap = (lambda c: nc - 1 - c) if rev else (lambda c: c)
    ins = [xbc, xbc, xbc, dt_rows, adt_rows, s0]
    specs = [
        pl.BlockSpec((1, q, wblk), lambda bi, gi, c: (bi, cmap(c), gi)),
        pl.BlockSpec((1, q, sblk), lambda bi, gi, c: (bi, cmap(c), b_off + gi)),
        pl.BlockSpec((1, q, sblk), lambda bi, gi, c: (bi, cmap(c), c_off + gi)),
        pl.BlockSpec((1, gpb, n_hg, q), lambda bi, gi, c: (bi, gi, 0, cmap(c))),
        pl.BlockSpec((1, gpb, n_hg, q), lambda bi, gi, c: (bi, gi, 0, cmap(c))),
        pl.BlockSpec((1, gpb, SSM_STATE, width), lambda bi, gi, c: (bi, gi, 0, 0)),
    ]
    if final:
        yprev, z, dskip, nw = final_args
        ins += [yprev, z, dskip, nw]
        specs += [
            pl.BlockSpec((1, q, wblk), lambda bi, gi, c: (bi, cmap(c), gi)),
            pl.BlockSpec((1, q, wblk), lambda bi, gi, c: (bi, cmap(c), gi)),
            pl.BlockSpec((1, wblk), lambda bi, gi, c: (0, gi)),
            pl.BlockSpec((1, wblk), lambda bi, gi, c: (0, gi)),
        ]
    out_shapes, out_specs = [], []
    if with_output:
        out_shapes.append(jax.ShapeDtypeStruct((b, t, d_inner), BF16 if final else F32))
        out_specs.append(pl.BlockSpec((1, q, wblk), lambda bi, gi, c: (bi, cmap(c), gi)))
    out_shapes.append(jax.ShapeDtypeStruct(s0.shape, F32))
    out_specs.append(pl.BlockSpec((1, gpb, SSM_STATE, width), lambda bi, gi, c: (bi, gi, 0, 0)))
    return pl.pallas_call(
        functools.partial(_ssd_kernel, rev=rev, with_output=with_output, final=final, nc=nc),
        grid=(b, ngb, nc),
        in_specs=specs,
        out_specs=out_specs,
        out_shape=out_shapes,
        scratch_shapes=[pltpu.VMEM((gpb, SSM_STATE, width), F32)],
        compiler_params=_cparams("parallel", "parallel", "arbitrary"),
        name="ssd_scan_%s%s" % ("bwd" if rev else "fwd", "" if with_output else "_state"),
    )(*ins)


def _ssd_project(h, nw, sc, sh, wz, wx, wdt, cw, cb, dt_bias, a_log, need_z, tag):
    b, t, d = h.shape
    d_inner = wz.shape[1]
    z = None
    if need_z:
        tn = _tile(d_inner, 512)
        shapes, specs = _std_out(b, t, d_inner, tn, F32)
        z = _proj(h, nw, sc, sh, [(wz, 0)], n_col_tiles=d_inner // tn, tn=tn, tm=1024, epi=_epi_identity,
                  out_shapes=shapes, out_specs=specs, name="ssd_in_z_" + tag)[0]
    nx = wx.shape[1]
    tn = _tile(nx, 512)
    shapes, specs = _std_out(b, t, nx, tn, F32)
    xbc = _proj(h, nw, sc, sh, [(wx, 0)], n_col_tiles=nx // tn, tn=tn, tm=1024, epi=_epi_silu,
                conv=(cw, cb), out_shapes=shapes, out_specs=specs, name="ssd_in_xbc_" + tag)[0]
    nd = wdt.shape[1]
    shapes = [jax.ShapeDtypeStruct((b, t, nd), F32)] * 2
    specs = lambda tm: [pl.BlockSpec((1, tm, nd), lambda bi, i, j: (bi, i, 0))] * 2
    dt, adt = _proj(h, nw, sc, sh, [(wdt, 0)], n_col_tiles=1, tn=nd, tm=1024, epi=_dt_epi,
                    out_shapes=shapes, out_specs=specs,
                    aux=[dt_bias.reshape(1, nd), a_log.reshape(1, nd)],
                    aux_specs=[pl.BlockSpec((1, nd), lambda bi, i, j: (0, 0))] * 2, name="ssd_in_dt_" + tag)

    def rows(v):
        return jnp.transpose(v.reshape(b, t, 2, SSM_GROUPS, -1), (2, 0, 3, 4, 1))

    return z, xbc, rows(dt), rows(adt)


def _ssd_mixer(h, hc, mods, mods_c, nw_pre, p, g, nw_post):
    b, t, d = h.shape
    wz, wx, wdt = p["wz"], p["wx"], p["wdt"]
    d_inner = wz.shape[1]
    sc, sh = mods
    scc, shc = mods_c
    z, xbc, dt_r, adt_r = _ssd_project(h, nw_pre, sc, sh, wz, wx, wdt, p["cw"], p["cb"], p["dt_bias"],
                                       p["a_log"], True, "lat")
    _, xbc_c, dt_rc, adt_rc = _ssd_project(hc, nw_pre, scc, shc, wz, wx, wdt, p["cw"], p["cb"], p["dt_bias"],
                                           p["a_log"], False, "ctx")
    s0 = jnp.zeros((b, SSM_GROUPS, SSM_STATE, d_inner // SSM_GROUPS), F32)
    (s_f,) = _ssd_scan(xbc_c, dt_rc[0], adt_rc[0], s0, d_inner, rev=False, with_output=False)
    (s_b,) = _ssd_scan(xbc_c, dt_rc[1], adt_rc[1], s0, d_inner, rev=True, with_output=False)
    y_f, _ = _ssd_scan(xbc, dt_r[0], adt_r[0], s_f, d_inner, rev=False, with_output=True)
    dskip = jnp.repeat(p["d_skip"], SSM_HEAD_DIM).reshape(1, d_inner)
    yn, _ = _ssd_scan(xbc, dt_r[1], adt_r[1], s_b, d_inner, rev=True, with_output=True,
                      final_args=(y_f, z, dskip, p["norm_w"].reshape(1, d_inner)))
    return _outproj(yn, p["w_out"], h, g, nw_post, name="ssd_out")


def _hyena_tables(n):
    half = n // 2
    k, s = _int_grid(half)
    def trig(m):
        ang = (m % (2 * n)).astype(F32) * (math.pi / n)
        return jnp.cos(ang), jnp.sin(ang)
    ce, se = trig(2 * s * k)
    co, so = trig((2 * s + 1) * k)
    cot, sot = trig((2 * k + 1) * s)
    alt_s = jnp.where(s % 2 == 0, 1.0, -1.0)
    alt_k = jnp.where(k % 2 == 0, 1.0, -1.0)
    sep = jnp.where(k == 0, alt_s, se)
    sop = jnp.where(k == 0, alt_s, so)
    g = jnp.where(k == 0, alt_s, -se)
    nsept = jnp.where(s == 0, -alt_k, -se)
    nsopt = jnp.where(s == 0, -alt_k, -sot)
    bf = lambda *ts: jnp.stack([t.astype(BF16) for t in ts])
    return dict(ce=bf(ce), co=bf(co), sep=bf(sep), sop=bf(sop), g=bf(g), nsop=bf(-sop),
                inv_c=bf(ce, cot), inv_s=bf(nsept, nsopt))


def _hyena_feats(n):
    t01 = jnp.linspace(0.0, 1.0, n, dtype=F32)[:, None]
    w = (2.0 * math.pi / n) * jnp.arange(n, dtype=F32)[:, None]
    f = jnp.linspace(1e-4, HY_BANDS - 1, HY_BANDS, dtype=F32)[None, :]
    feats = jnp.concatenate([t01, jnp.cos(f * w), -jnp.sin(f * w)], axis=-1)
    return jnp.pad(feats, ((0, 0), (0, LANES - HY_EMB)))


def _taps_kernel(feats_ref, w1_ref, b1_ref, w2_ref, b2_ref, fr_ref, w3_ref, dec_ref, o_ref, hid_ref, *,
                 tiles_per_dir):
    j = pl.program_id(0)

    @pl.when(j == 0)
    def _():
        fr = fr_ref[...]
        h1 = jnp.sin(fr * (_dot3(feats_ref[...], w1_ref[...]) + b1_ref[...]))
        hid_ref[...] = jnp.sin(fr * (_dot3(h1, w2_ref[...]) + b2_ref[...]))

    taps = _dot3(hid_ref[...], w3_ref[...])
    taps = taps * jnp.exp(-feats_ref[:, 0:1] * jnp.abs(dec_ref[...]))
    is_bwd = (j // tiles_per_dir) % 2 == 1
    row = lax.broadcasted_iota(jnp.int32, taps.shape, 0)
    o_ref[0] = jnp.where((row == 0) & is_bwd, 0.0, taps).astype(o_ref.dtype)


def _hyena_taps(n, p, d):
    feats = _hyena_feats(n)
    tn = _tile(d, 256)
    tpd = d // tn
    ncols = HY_ORDER * 2 * d

    def omap(j):
        return ((j // tpd) % 2, 0, (j // (2 * tpd)) * tpd + j % tpd)

    full = lambda j: (0, 0)
    return pl.pallas_call(
        functools.partial(_taps_kernel, tiles_per_dir=tpd),
        grid=(ncols // tn,),
        in_specs=[
            pl.BlockSpec((n, LANES), full),
            pl.BlockSpec((LANES, LANES), full), pl.BlockSpec((1, LANES), full),
            pl.BlockSpec((LANES, LANES), full), pl.BlockSpec((1, LANES), full),
            pl.BlockSpec((1, LANES), full),
            pl.BlockSpec((LANES, tn), lambda j: (0, j)),
            pl.BlockSpec((1, tn), lambda j: (0, j)),
        ],
        out_specs=pl.BlockSpec((1, n, tn), omap),
        out_shape=jax.ShapeDtypeStruct((2, n, HY_ORDER * d), BF16),
        scratch_shapes=[pltpu.VMEM((n, LANES), F32)],
        compiler_params=_cparams("arbitrary"),
        name="hyena_taps",
    )(feats, p["fw1"], p["fb1"], p["fw2"], p["fb2"], p["ffreq"], p["fw3"], p["decay"])


def _first_row(shape, i):
    return (lax.broadcasted_iota(jnp.int32, shape, 0) == 0) & (i == 0)


def _hyena_spectrum(taps, tabs, n, d):
    half = n // 2
    ncol = HY_ORDER * d
    taps2 = taps.reshape(2, half, 2 * ncol)
    tn = _tile(ncol, 512)
    nj = ncol // tn
    views = [(taps2, lambda p, b, dr=dr: dr, lambda p, j, par=par: par * nj + j)
             for dr in (0, 1) for par in (0, 1)]
    l_tabs = [tabs["ce"], tabs["co"], tabs["sep"], tabs["sop"], tabs["g"], tabs["nsop"]]
    terms = [(0, 0, 0), (0, 0, 2), (1, 1, 1), (1, 1, 3), (2, 2, 0), (2, 4, 2), (3, 3, 1), (3, 5, 3)]

    def epi(i, accs, aux_refs, out_refs):
        pec, poc, pes, pos = accs
        first = _first_row(pec.shape, i)
        out_refs[0][...] = pec + poc
        out_refs[1][...] = jnp.where(first, pes, -(pes + pos))
        out_refs[2][...] = pec - poc
        out_refs[3][...] = jnp.where(first, -pos, pes - pos)

    return _dft_call(
        l_tabs, views, terms, 4, n_par=1, batch=1, m=half, kdim=half, n=ncol, tm=256, tn=tn, epi=epi,
        out_shapes=[jax.ShapeDtypeStruct((half, ncol), F32)] * 4,
        out_specs=lambda tm, tn_: [pl.BlockSpec((tm, tn_), lambda p, i, j, b: (i, j))] * 4,
        name="hyena_spectrum")


def _hyena_long_conv(src_bf, src_f32, gate, gate_off, spec, order, bias, tabs, d, out_dtypes, tag):
    b2, half, _ = src_bf.shape
    b = b2 // 2
    n = 2 * half
    inv_n = 1.0 / n
    tn = _tile(d, 512)
    nj = d // tn

    def fwd_epi(i, accs, aux_refs, out_refs):
        pec, poc, pes, pos = accs
        hr, hi, hur, hui = [a[...] for a in aux_refs]
        vr, ur = pec + poc, pec - poc
        vi, ui = -(pes + pos), pes - pos
        yr, yi = vr * hr - vi * hi, vr * hi + vi * hr
        yur, yui = ur * hur - ui * hui, ur * hui + ui * hur
        first = _first_row(pec.shape, i)
        y0, yny = vr * hr, ur * hur
        mr, mi = pes, -pos
        ymr, ymi = mr * hi - mi * hui, mr * hui + mi * hi
        ae = jnp.where(first, 0.5 * (y0 + yny), yr + yur)
        be = jnp.where(first, -ymr, yi - yui)
        ao = jnp.where(first, 0.5 * (y0 - yny), yr - yur)
        bo = jnp.where(first, ymi, yi + yui)
        for q, v in enumerate((ae, be, ao, bo)):
            out_refs[0][0, :, q * tn:(q + 1) * tn] = (v * inv_n).astype(BF16)

    views = [(src_bf, lambda p, bi: bi, lambda p, j: j), (src_bf, lambda p, bi: b + bi, lambda p, j: j)]
    yab = _dft_call(
        [tabs["ce"], tabs["co"], tabs["sep"], tabs["sop"]], views,
        [(0, 0, 0), (1, 1, 1), (2, 2, 0), (3, 3, 1)], 4,
        n_par=1, batch=b, m=half, kdim=half, n=d, tm=512, tn=tn, epi=fwd_epi,
        out_shapes=[jax.ShapeDtypeStruct((b, half, 4 * d), BF16)],
        out_specs=lambda tm, tn_: [pl.BlockSpec((1, tm, 4 * tn_), lambda p, i, j, bi: (bi, i, j))],
        aux=list(spec),
        aux_specs=lambda tm, tn_: [pl.BlockSpec((tm, tn_), lambda p, i, j, bi: (i, order * nj + j))] * 4,
        name="hyena_fwd_" + tag)[0]

    def inv_epi(i, accs, aux_refs, out_refs):
        val = aux_refs[1][0] * (accs[0] + aux_refs[0][0] * aux_refs[2][...])
        for o_ref in out_refs:
            o_ref[0] = val.astype(o_ref.dtype)

    return _dft_call(
        [tabs["inv_c"], tabs["inv_s"]],
        [(yab, lambda p, bi: bi, lambda p, j: 4 * j + 2 * p), (yab, lambda p, bi: bi, lambda p, j: 4 * j + 2 * p + 1)],
        [(0, 0, 0), (0, 1, 1)], 1,
        n_par=2, batch=b, m=half, kdim=half, n=d, tm=1024, tn=tn, epi=inv_epi,
        out_shapes=[jax.ShapeDtypeStruct((2 * b, half, d), dt) for dt in out_dtypes],
        out_specs=lambda tm, tn_: [pl.BlockSpec((1, tm, tn_), lambda p, i, j, bi: (p * b + bi, i, j))
                                   for _ in out_dtypes],
        aux=[src_f32, gate, bias],
        aux_specs=lambda tm, tn_: [
            pl.BlockSpec((1, tm, tn_), lambda p, i, j, bi: (p * b + bi, i, j)),
            pl.BlockSpec((1, tm, tn_), lambda p, i, j, bi: (p * b + bi, i, gate_off * nj + j)),
            pl.BlockSpec((1, tn_), lambda p, i, j, bi: (0, j)),
        ],
        name="hyena_inv_" + tag)


def _hyena_mixer(h, mods, nw_pre, p, g, nw_post):
    b, t, d = h.shape
    sc, sh = mods
    tn = _tile(3 * d, 512)
    half = t // 2
    shapes, specs = _parity_out(b, t, 3 * d, tn, [F32, BF16])
    pj, pj_bf = _proj(h, nw_pre, sc, sh, [(p["w_in"], 0)], n_col_tiles=3 * d // tn, tn=tn, tm=1024,
                      epi=_epi_identity, conv=(p["cw"], p["cb"]), out_shapes=shapes, out_specs=specs,
                      parity=True, name="hyena_in")
    tabs = _hyena_tables(t)
    taps = _hyena_taps(t, p, d)
    spec = _hyena_spectrum(taps, tabs, t, d)
    fbias = p["fbias"]
    pj2 = pj.reshape(2 * b, half, 3 * d)
    z, z_bf = _hyena_long_conv(pj_bf.reshape(2 * b, half, 3 * d), pj2, pj2, 1, spec, 0, fbias[0:1], tabs, d,
                               (F32, BF16), "o1")
    (y,) = _hyena_long_conv(z_bf, z, pj2, 2, spec, 1, fbias[1:2], tabs, d, (BF16,), "o2")
    return _outproj(y.reshape(2, b, half, d), p["w_out"], h, g, nw_post, parity=True, name="hyena_out")


def _conv_ffn(h, sc, sh, g, nw_pre, nw_post, w_up_bf, cw, cb, w_down_bf, tag):
    b, t, d = h.shape
    dff = w_down_bf.shape[0]
    tn = _tile(dff, 512)
    nj = dff // tn
    shapes, specs = _std_out(b, t, dff, tn, BF16)
    act = _proj(h, nw_pre, sc, sh, [(w_up_bf, 0), (w_up_bf, nj)], n_col_tiles=nj, tn=tn, tm=1024,
                epi=_epi_swiglu, conv=(cw, cb.reshape(1, -1)), out_shapes=shapes, out_specs=specs,
                name="ffn_up_" + tag)[0]
    return _outproj(act, w_down_bf, h, g, nw_post, name="ffn_down_" + tag)


def kernel(x, c, ctx, c_ctx, ada_w, ada_b, norm_pre_mix, norm_post_mix, norm_pre_ffn, norm_post_ffn, ffn_up, ffn_conv_w, ffn_conv_b, ffn_down, fn_in, fn_out, attn_in, attn_q_gain, attn_k_gain, attn_out, ssm_in, ssm_conv_w, ssm_conv_b, ssm_dt_bias, ssm_a_log, ssm_d, ssm_norm, ssm_out, hy_in, hy_conv_w, hy_conv_b, hy_f_w1, hy_f_b1, hy_f_w2, hy_f_b2, hy_f_w3, hy_f_freq, hy_decay, hy_f_bias, hy_out):
    b, t, d = x.shape
    tc = ctx.shape[1]
    depth = ada_w.shape[0]
    readers = [i for i in range(depth) if i % N_MIXERS in CTX_READER_KINDS]
    last_reader = max(readers) if readers else -1

    n_rows = -(-(b + 1) // 8) * 8
    c_rows = jnp.zeros((n_rows, d), F32).at[:b].set(c).at[b].set(c_ctx)
    mod = _modulation_all(c_rows, ada_w, ada_b).reshape(depth, n_rows, 6, d)

    h, hc = x, ctx
    for i in range(depth):
        kind, j = i % N_MIXERS, i // N_MIXERS
        ctx_live = i <= last_reader
        ctx_next = i < last_reader
        m_lat = [mod[i, :b, s][:, None, :] for s in range(6)]
        sh1, sc1, g1, sh2, sc2, g2 = m_lat
        if ctx_live:
            m_ctx = [jnp.broadcast_to(mod[i, b, s][None, None, :], (b, 1, d)) for s in range(6)]
            csh1, csc1, cg1, csh2, csc2, cg2 = m_ctx
        nw_pre, nw_post = norm_pre_mix[i], norm_post_mix[i]
        if kind == 0:
            gdim = d // FN_GROUPS
            cc, scs = _fourier_chan_tables(gdim)
            w_fold = _fold_channel_dft(fn_in[j], cc, scs)
            w_out_bf = fn_out[j].astype(BF16)
            h = _fourier_mixer(h, nw_pre, sc1, sh1, w_fold, _fourier_pos_tables(t), w_out_bf, g1, nw_post, "lat")
            if ctx_next:
                hc = _fourier_mixer(hc, nw_pre, csc1, csh1, w_fold, _fourier_pos_tables(tc), w_out_bf, cg1,
                                    nw_post, "ctx")
        elif kind == 1:
            w_in_bf = attn_in[j].astype(BF16)
            w_out_bf = attn_out[j].astype(BF16)
            n_q, n_kv = d // HEAD_DIM, N_KV_HEADS
            gains = jnp.concatenate([jnp.tile(attn_q_gain[j], n_q), jnp.tile(attn_k_gain[j], n_kv)]).reshape(1, -1)
            qkv = _attn_project(h, nw_pre, sc1, sh1, w_in_bf, gains, _rope_tables(t), "lat")
            qkv_c = _attn_project(hc, nw_pre, csc1, csh1, w_in_bf, gains, None, "ctx")
            o = _attention(qkv, qkv_c, d)
            h = _outproj(o, w_out_bf, h, g1, nw_post, name="attn_out_lat")
            if ctx_next:
                oc = _attention(None, qkv_c, d)
                hc = _outproj(oc, w_out_bf, hc, cg1, nw_post, name="attn_out_ctx")
        elif kind == 2:
            w = ssm_in[j]
            d_inner = ssm_out.shape[1]
            n_conv = ssm_conv_w.shape[2]
            p = dict(wz=w[:, :d_inner].astype(BF16), wx=w[:, d_inner:d_inner + n_conv].astype(BF16),
                     wdt=w[:, d_inner + n_conv:].astype(BF16), cw=ssm_conv_w[j], cb=ssm_conv_b[j].reshape(1, -1),
                     dt_bias=ssm_dt_bias[j], a_log=ssm_a_log[j], d_skip=ssm_d[j], norm_w=ssm_norm[j],
                     w_out=ssm_out[j].astype(BF16))
            if ctx_next:
                raise NotImplementedError("context output of the SSD mixer is not needed at this depth")
            h = _ssd_mixer(h, hc, (sc1, sh1), (csc1, csh1), nw_pre, p, g1, nw_post)
        else:
            pad_r = LANES - hy_f_w1.shape[1]
            pad_c = LANES - hy_f_w1.shape[2]
            p = dict(w_in=hy_in[j].astype(BF16), cw=hy_conv_w[j], cb=hy_conv_b[j].reshape(1, -1),
                     fw1=jnp.pad(hy_f_w1[j], ((0, pad_r), (0, pad_c))),
                     fb1=jnp.pad(hy_f_b1[j], (0, pad_c)).reshape(1, LANES),
                     fw2=jnp.pad(hy_f_w2[j], ((0, pad_c), (0, pad_c))),
                     fb2=jnp.pad(hy_f_b2[j], (0, pad_c)).reshape(1, LANES),
                     ffreq=jnp.pad(hy_f_freq[j], (0, pad_c)).reshape(1, LANES),
                     fw3=jnp.pad(hy_f_w3[j], ((0, pad_c), (0, 0))), decay=hy_decay[j].reshape(1, -1),
                     fbias=hy_f_bias[j], w_out=hy_out[j].astype(BF16))
            if ctx_next:
                raise NotImplementedError("context output of the Hyena mixer is not needed at this depth")
            h = _hyena_mixer(h, (sc1, sh1), nw_pre, p, g1, nw_post)
        w_up_bf = ffn_up[i].astype(BF16)
        w_down_bf = ffn_down[i].astype(BF16)
        h = _conv_ffn(h, sc2, sh2, g2, norm_pre_ffn[i], norm_post_ffn[i], w_up_bf, ffn_conv_w[i], ffn_conv_b[i],
                      w_down_bf, "lat")
        if ctx_next:
            hc = _conv_ffn(hc, csc2, csh2, cg2, norm_pre_ffn[i], norm_post_ffn[i], w_up_bf, ffn_conv_w[i],
                           ffn_conv_b[i], w_down_bf, "ctx")
    return h
```

```python
import functools
import math

import jax
import jax.numpy as jnp
from jax import lax
from jax.experimental import pallas as pl
from jax.experimental.pallas import tpu as pltpu

F32 = jnp.float32
BF16 = jnp.bfloat16

GRID_W = 64
N_MIXERS = 4
CTX_READER_KINDS = (1, 2)
RMS_EPS = 1e-6
FN_GROUPS = 4
HEAD_DIM = 128
N_KV_HEADS = 4
ROPE_THETA = 10000.0
ROPE_FREQS = HEAD_DIM // 4
ATTN_SCALE = HEAD_DIM ** -0.5
SSM_HEAD_DIM = 64
SSM_GROUPS = 8
SSM_STATE = 128
SSM_CHUNK = 128
SSD_GROUPS_PER_STEP = 8
HY_ORDER = 2
HY_EMB = 33
HY_BANDS = (HY_EMB - 1) // 2

V7X_VMEM_LIMIT_BYTES = 56 * 1024 * 1024
LANES = 128
CONV_HALO = 16
PROJ_ROW_CHUNK = 128
NEG_BIG = -1e30


def _cparams(*sem):
    return pltpu.CompilerParams(dimension_semantics=sem, vmem_limit_bytes=V7X_VMEM_LIMIT_BYTES)


def _tile(n, pref, mult=LANES):
    if n <= pref:
        return n
    t = (pref // mult) * mult
    while t >= mult:
        if n % t == 0:
            return t
        t -= mult
    return n


def _silu(x):
    return x * (1.0 / (1.0 + jnp.exp(-x)))


def _split_bf16(v):
    hi = v.astype(BF16)
    lo = (v - hi.astype(F32)).astype(BF16)
    return hi, lo


def _dot(a, b):
    return jnp.dot(a, b, preferred_element_type=F32)


def _dot_nt(a, b):
    return lax.dot_general(a, b, (((1,), (1,)), ((), ())), preferred_element_type=F32)


def _dot3(a, b):
    ah, al = _split_bf16(a)
    bh, bl = _split_bf16(b)
    return _dot(ah, bh) + _dot(ah, bl) + _dot(al, bh)


def _rms_rows(x, w):
    return x * lax.rsqrt(jnp.mean(x * x, axis=-1, keepdims=True) + RMS_EPS) * w


def _mod_kernel(c_ref, w_ref, b_ref, o_ref):
    a = _silu(c_ref[...]).astype(BF16)
    o_ref[0] = _dot(a, w_ref[0].astype(BF16)) + b_ref[0]


def _modulation_all(c_rows, ada_w, ada_b):
    depth, d, n6 = ada_w.shape
    r = c_rows.shape[0]
    tn = _tile(n6, 1024)
    return pl.pallas_call(
        _mod_kernel,
        grid=(depth, n6 // tn),
        in_specs=[
            pl.BlockSpec((r, d), lambda l, j: (0, 0)),
            pl.BlockSpec((1, d, tn), lambda l, j: (l, 0, j)),
            pl.BlockSpec((1, 1, tn), lambda l, j: (l, 0, j)),
        ],
        out_specs=pl.BlockSpec((1, r, tn), lambda l, j: (l, 0, j)),
        out_shape=jax.ShapeDtypeStruct((depth, r, n6), F32),
        compiler_params=_cparams("parallel", "parallel"),
        name="ada_mod",
    )(c_rows, ada_w, ada_b.reshape(depth, 1, n6))


def _proj_prologue(h_ref, halo_refs, nw_ref, sc_ref, sh_ref, u_ref, i, tm, n_row_tiles):
    hl = CONV_HALO
    off = hl if halo_refs else 0
    rc = _tile(tm, PROJ_ROW_CHUNK, 16)
    wv = nw_ref[...] * (1.0 + sc_ref[0])
    shv = sh_ref[0]

    def nm(x):
        return x * lax.rsqrt(jnp.mean(x * x, axis=-1, keepdims=True) + RMS_EPS) * wv + shv

    x = h_ref[0]
    inv = lax.rsqrt(jnp.mean(x * x, axis=-1, keepdims=True) + RMS_EPS)
    for r0 in range(0, tm, rc):
        u_ref[off + r0:off + r0 + rc, :] = (h_ref[0, r0:r0 + rc, :] * inv[r0:r0 + rc] * wv + shv).astype(BF16)
    if halo_refs:
        hp_ref, hn_ref = halo_refs
        u_ref[0:hl, :] = jnp.where(i == 0, 0.0, nm(hp_ref[0])).astype(BF16)
        u_ref[hl + tm:, :] = jnp.where(i == n_row_tiles - 1, 0.0, nm(hn_ref[0])).astype(BF16)


def _proj_kernel(*refs, n_w, use_conv, n_aux, n_out, epi, tm, n_row_tiles, parity):
    refs = list(refs)
    h_ref = refs.pop(0)
    halo_refs = (refs.pop(0), refs.pop(0)) if use_conv else ()
    nw_ref, sc_ref, sh_ref = refs.pop(0), refs.pop(0), refs.pop(0)
    w_refs = [refs.pop(0) for _ in range(n_w)]
    cw_refs, cb_refs = [], []
    if use_conv:
        for _ in range(n_w):
            cw_refs.append(refs.pop(0))
            cb_refs.append(refs.pop(0))
    aux_refs = [refs.pop(0) for _ in range(n_aux)]
    out_refs = [refs.pop(0) for _ in range(n_out)]
    u_ref = refs.pop(0)
    p_ref = refs.pop(0) if parity else None
    i = pl.program_id(1)
    j = pl.program_id(2)
    hl = CONV_HALO
    off = hl if use_conv else 0

    @pl.when(j == 0)
    def _():
        _proj_prologue(h_ref, halo_refs, nw_ref, sc_ref, sh_ref, u_ref, i, tm, n_row_tiles)

    tn = w_refs[0].shape[1]
    cols = slice(0, tn)
    if parity:
        hm = tm // 2
        for k in range(n_w):
            p = _dot(u_ref[...], w_refs[k][...])
            for sl in range(tn // LANES):
                p_ref[k, sl] = p[:, sl * LANES:(sl + 1) * LANES]
        for par in (0, 1):
            ys = []
            for k in range(n_w):
                slabs = []
                for sl in range(tn // LANES):
                    lanes = slice(sl * LANES, (sl + 1) * LANES)
                    rows = lambda shift: p_ref[k, sl, pl.ds(off + par + shift, hm, stride=2), :]
                    if use_conv:
                        cw = cw_refs[k][:, lanes]
                        slabs.append(cw[0:1] * rows(-1) + cw[1:2] * rows(0) + cw[2:3] * rows(1)
                                     + cb_refs[k][:, lanes])
                    else:
                        slabs.append(rows(0))
                ys.append(jnp.concatenate(slabs, axis=1))
            epi(j, ys, aux_refs, [o.at[par] for o in out_refs], slice(0, hm), cols)
        return
    ys = []
    for k in range(n_w):
        p = _dot(u_ref[...], w_refs[k][...])
        if use_conv:
            cw = cw_refs[k][...]
            prev = pltpu.roll(p, 1, axis=0)
            nxt = pltpu.roll(p, tm + 2 * hl - 1, axis=0)
            p = (cw[0:1] * prev + cw[1:2] * p + cw[2:3] * nxt + cb_refs[k][...])[hl:hl + tm]
        ys.append(p)
    epi(j, ys, aux_refs, out_refs, slice(0, tm), cols)


def _proj_piped_kernel(h_ref, nw_ref, sc_ref, sh_ref, w_ref, *refs, n_aux, epi, tm, n_row_tiles, n_col_tiles):
    aux_refs = list(refs[:n_aux])
    o_ref, u_ref, pa_ref, pb_ref = refs[n_aux:]
    s = pl.program_id(1)
    n_pairs = n_row_tiles * n_col_tiles
    cur = jnp.minimum(s, n_pairs - 1)
    j_prev = jnp.maximum(s - 1, 0) % n_col_tiles
    tn = w_ref.shape[1]

    @pl.when((cur % n_col_tiles == 0) & (s < n_pairs))
    def _():
        _proj_prologue(h_ref, (), nw_ref, sc_ref, sh_ref, u_ref, cur // n_col_tiles, tm, n_row_tiles)

    @pl.when(s == 0)
    def _():
        pb_ref[...] = jnp.zeros(pb_ref.shape, F32)

    def body(store_ref, load_ref):
        for sl in range(tn // LANES):
            epi(j_prev, [load_ref[sl]], aux_refs, [o_ref], slice(0, tm), slice(sl * LANES, (sl + 1) * LANES))
        p = _dot(u_ref[...], w_ref[...])
        for sl in range(tn // LANES):
            store_ref[sl] = p[:, sl * LANES:(sl + 1) * LANES]

    @pl.when(s % 2 == 0)
    def _():
        body(pa_ref, pb_ref)

    @pl.when(s % 2 == 1)
    def _():
        body(pb_ref, pa_ref)


def _proj(h, nw, sc, sh, w_views, *, n_col_tiles, tn, tm, epi, out_shapes, out_specs,
          conv=None, aux=(), aux_specs=(), parity=False, name):
    b, t, d = h.shape
    tm = _tile(t, tm, 16)
    n_row_tiles = t // tm
    use_conv = conv is not None
    hl = CONV_HALO
    ins = [h]
    specs = [pl.BlockSpec((1, tm, d), lambda bi, i, j: (bi, i, 0))]
    if use_conv:
        r = tm // hl
        last = t // hl - 1
        ins += [h, h]
        specs += [
            pl.BlockSpec((1, hl, d), lambda bi, i, j: (bi, jnp.maximum(i * r - 1, 0), 0)),
            pl.BlockSpec((1, hl, d), lambda bi, i, j: (bi, jnp.minimum((i + 1) * r, last), 0)),
        ]
    ins += [nw.reshape(1, d), sc, sh]
    specs += [
        pl.BlockSpec((1, d), lambda bi, i, j: (0, 0)),
        pl.BlockSpec((1, 1, d), lambda bi, i, j: (bi, 0, 0)),
        pl.BlockSpec((1, 1, d), lambda bi, i, j: (bi, 0, 0)),
    ]
    for w, off in w_views:
        ins.append(w)
        specs.append(pl.BlockSpec((d, tn), lambda bi, i, j, off=off: (0, off + j)))
    if use_conv:
        cw, cb = conv
        for _, off in w_views:
            ins += [cw, cb]
            specs += [
                pl.BlockSpec((cw.shape[0], tn), lambda bi, i, j, off=off: (0, off + j)),
                pl.BlockSpec((1, tn), lambda bi, i, j, off=off: (0, off + j)),
            ]
    ins += list(aux)
    specs += list(aux_specs)
    rows = tm + 2 * hl if use_conv else tm
    scratch = [pltpu.VMEM((rows, d), BF16)]
    if parity:
        scratch.append(pltpu.VMEM((len(w_views), tn // LANES, rows, LANES), F32))
    kern = functools.partial(_proj_kernel, n_w=len(w_views), use_conv=use_conv, n_aux=len(aux),
                             n_out=len(out_shapes), epi=epi, tm=tm, n_row_tiles=n_row_tiles, parity=parity)
    return pl.pallas_call(
        kern,
        grid=(b, n_row_tiles, n_col_tiles),
        in_specs=specs,
        out_specs=out_specs(tm),
        out_shape=out_shapes,
        scratch_shapes=scratch,
        compiler_params=_cparams("parallel", "parallel", "arbitrary"),
        name=name,
    )(*ins)


def _proj_piped(h, nw, sc, sh, w, *, n_col_tiles, tn, tm, epi, out_shape, out_spec, aux, aux_specs, name):
    b, t, d = h.shape
    n_row_tiles = t // tm
    n_pairs = n_row_tiles * n_col_tiles
    cur = lambda s: jnp.minimum(s, n_pairs - 1)
    prv = lambda s: jnp.maximum(s - 1, 0)

    def finished(spec):
        return pl.BlockSpec(spec.block_shape,
                            lambda bi, s, f=spec.index_map: f(bi, prv(s) // n_col_tiles, prv(s) % n_col_tiles))

    specs = [
        pl.BlockSpec((1, tm, d), lambda bi, s: (bi, cur(s) // n_col_tiles, 0)),
        pl.BlockSpec((1, d), lambda bi, s: (0, 0)),
        pl.BlockSpec((1, 1, d), lambda bi, s: (bi, 0, 0)),
        pl.BlockSpec((1, 1, d), lambda bi, s: (bi, 0, 0)),
        pl.BlockSpec((d, tn), lambda bi, s: (0, cur(s) % n_col_tiles)),
    ] + [finished(sp) for sp in aux_specs]
    park = pltpu.VMEM((tn // LANES, tm, LANES), F32)
    return pl.pallas_call(
        functools.partial(_proj_piped_kernel, n_aux=len(aux), epi=epi, tm=tm, n_row_tiles=n_row_tiles,
                          n_col_tiles=n_col_tiles),
        grid=(b, n_pairs + 1),
        in_specs=specs,
        out_specs=finished(out_spec),
        out_shape=out_shape,
        scratch_shapes=[pltpu.VMEM((tm, d), BF16), park, park],
        compiler_params=_cparams("parallel", "arbitrary"),
        name=name,
    )(h, nw.reshape(1, d), sc, sh, w, *aux)


def _std_out(b, t, n, tn, dtype):
    shapes = [jax.ShapeDtypeStruct((b, t, n), dtype)]
    specs = lambda tm: [pl.BlockSpec((1, tm, tn), lambda bi, i, j: (bi, i, j))]
    return shapes, specs


def _parity_out(b, t, n, tn, dtypes):
    shapes = [jax.ShapeDtypeStruct((2, b, t // 2, n), dt) for dt in dtypes]
    specs = lambda tm: [pl.BlockSpec((2, 1, tm // 2, tn), lambda bi, i, j: (0, bi, i, j)) for _ in dtypes]
    return shapes, specs


def _epi_identity(j, ys, aux_refs, out_refs, rows, cols):
    for o_ref in out_refs:
        o_ref[0, rows, cols] = ys[0].astype(o_ref.dtype)


def _epi_silu(j, ys, aux_refs, out_refs, rows, cols):
    out_refs[0][0, rows, cols] = _silu(ys[0]).astype(out_refs[0].dtype)


def _epi_swiglu(j, ys, aux_refs, out_refs, rows, cols):
    out_refs[0][0, rows, cols] = (_silu(ys[0]) * ys[1]).astype(out_refs[0].dtype)


def _outproj_kernel(a_ref, w_ref, h_ref, g_ref, nw_ref, o_ref, acc_ref, *, nj, tn):
    j = pl.program_id(2)
    acc_ref[j] = _dot(a_ref[0], w_ref[...])

    @pl.when(j == nj - 1)
    def _():
        ss = None
        for jj in range(nj):
            o = acc_ref[jj]
            s = jnp.sum(o * o, axis=-1, keepdims=True)
            ss = s if ss is None else ss + s
        inv = lax.rsqrt(ss * (1.0 / (nj * tn)) + RMS_EPS)
        for jj in range(nj):
            cols = slice(jj * tn, (jj + 1) * tn)
            o_ref[0, :, cols] = h_ref[0, :, cols] + g_ref[0, :, cols] * (acc_ref[jj] * inv * nw_ref[:, cols])


def _outproj_parity_kernel(a_ref, w_ref, h_ref, g_ref, nw_ref, o_ref, acc_ref, *, nj, tn, hm):
    j = pl.program_id(2)
    spt = tn // LANES
    for par in (0, 1):
        r = _dot(a_ref[par, 0], w_ref[...])
        for sl in range(spt):
            acc_ref[j * spt + sl, pl.ds(par, hm, stride=2), :] = r[:, sl * LANES:(sl + 1) * LANES]

    @pl.when(j == nj - 1)
    def _():
        ss = None
        for sl in range(nj * spt):
            o = acc_ref[sl]
            s = jnp.sum(o * o, axis=-1, keepdims=True)
            ss = s if ss is None else ss + s
        inv = lax.rsqrt(ss * (1.0 / (nj * tn)) + RMS_EPS)
        for sl in range(nj * spt):
            cols = slice(sl * LANES, (sl + 1) * LANES)
            o_ref[0, :, cols] = h_ref[0, :, cols] + g_ref[0, :, cols] * (acc_ref[sl] * inv * nw_ref[:, cols])


def _outproj(a, w, h, g, nw, *, tm=512, tn=512, parity=False, name):
    b, t, d = h.shape
    kdim = w.shape[0]
    tm = _tile(t, tm, 16)
    tn = _tile(d, tn)
    nj = d // tn
    if parity:
        hm = tm // 2
        return pl.pallas_call(
            functools.partial(_outproj_parity_kernel, nj=nj, tn=tn, hm=hm),
            grid=(b, t // tm, nj),
            in_specs=[
                pl.BlockSpec((2, 1, hm, kdim), lambda bi, i, j: (0, bi, i, 0)),
                pl.BlockSpec((kdim, tn), lambda bi, i, j: (0, j)),
                pl.BlockSpec((1, tm, d), lambda bi, i, j: (bi, i, 0)),
                pl.BlockSpec((1, 1, d), lambda bi, i, j: (bi, 0, 0)),
                pl.BlockSpec((1, d), lambda bi, i, j: (0, 0)),
            ],
            out_specs=pl.BlockSpec((1, tm, d), lambda bi, i, j: (bi, i, 0)),
            out_shape=jax.ShapeDtypeStruct((b, t, d), F32),
            scratch_shapes=[pltpu.VMEM((d // LANES, tm, LANES), F32)],
            compiler_params=_cparams("parallel", "parallel", "arbitrary"),
            name=name,
        )(a, w, h, g, nw.reshape(1, d))
    return pl.pallas_call(
        functools.partial(_outproj_kernel, nj=nj, tn=tn),
        grid=(b, t // tm, nj),
        in_specs=[
            pl.BlockSpec((1, tm, kdim), lambda bi, i, j: (bi, i, 0)),
            pl.BlockSpec((kdim, tn), lambda bi, i, j: (0, j)),
            pl.BlockSpec((1, tm, d), lambda bi, i, j: (bi, i, 0)),
            pl.BlockSpec((1, 1, d), lambda bi, i, j: (bi, 0, 0)),
            pl.BlockSpec((1, d), lambda bi, i, j: (0, 0)),
        ],
        out_specs=pl.BlockSpec((1, tm, d), lambda bi, i, j: (bi, i, 0)),
        out_shape=jax.ShapeDtypeStruct((b, t, d), F32),
        scratch_shapes=[pltpu.VMEM((nj, tm, tn), F32)],
        compiler_params=_cparams("parallel", "parallel", "arbitrary"),
        name=name,
    )(a, w, h, g, nw.reshape(1, d))


def _dft_kernel(*refs, n_l, n_r, n_acc, terms, n_aux, n_out, epi):
    refs = list(refs)
    l_refs = [refs.pop(0) for _ in range(n_l)]
    r_refs = [refs.pop(0) for _ in range(n_r)]
    aux_refs = [refs.pop(0) for _ in range(n_aux)]
    out_refs = [refs.pop(0) for _ in range(n_out)]
    accs = [None] * n_acc
    for o, l, r in terms:
        dd = _dot(l_refs[l][0], r_refs[r][0])
        accs[o] = dd if accs[o] is None else accs[o] + dd
    epi(pl.program_id(1), accs, aux_refs, out_refs)


def _dft_call(l_tabs, r_views, terms, n_acc, *, n_par, batch, m, kdim, n, tm, tn, epi, out_shapes, out_specs,
              aux=(), aux_specs=None, name):
    tm = _tile(m, tm, 16)
    tn = _tile(n, tn)
    ins, specs = [], []
    for lt in l_tabs:
        ins.append(lt)
        specs.append(pl.BlockSpec((1, tm, kdim), lambda p, i, j, b: (p, i, 0)))
    for arr, lead_fn, col_fn in r_views:
        ins.append(arr)
        specs.append(pl.BlockSpec((1, kdim, tn),
                                  lambda p, i, j, b, lf=lead_fn, cf=col_fn: (lf(p, b), 0, cf(p, j))))
    ins += list(aux)
    if aux_specs is not None:
        specs += list(aux_specs(tm, tn))
    kern = functools.partial(_dft_kernel, n_l=len(l_tabs), n_r=len(r_views), n_acc=n_acc, terms=terms,
                             n_aux=len(aux), n_out=len(out_shapes), epi=epi)
    return pl.pallas_call(
        kern,
        grid=(n_par, m // tm, n // tn, batch),
        in_specs=specs,
        out_specs=out_specs(tm, tn),
        out_shape=out_shapes,
        compiler_params=_cparams("parallel", "parallel", "parallel", "parallel"),
        name=name,
    )(*ins)


def _int_grid(n):
    k = lax.broadcasted_iota(jnp.int32, (n, n), 0)
    j = lax.broadcasted_iota(jnp.int32, (n, n), 1)
    return k, j


def _fourier_pos_tables(n):
    k, j = _int_grid(n // 2)
    tabs = []
    for par in (0, 1):
        ang = ((k * (2 * j + par)) % n).astype(F32) * (2.0 * math.pi / n)
        tabs += [jnp.cos(ang).astype(BF16)[None], (-jnp.sin(ang)).astype(BF16)[None]]
    return tabs


def _fourier_chan_tables(gdim):
    kc, jc = _int_grid(gdim)
    angc = ((kc * jc) % gdim).astype(F32) * (2.0 * math.pi / gdim)
    return jnp.cos(angc), jnp.sin(angc)


def _wfold_kernel(w_ref, c_ref, s_ref, oc_ref, os_ref):
    w = w_ref[...]
    oc_ref[...] = _dot3(w, c_ref[...]).astype(BF16)
    os_ref[...] = _dot3(w, s_ref[...]).astype(BF16)


def _fold_channel_dft(w_in, cc, sc):
    d = w_in.shape[0]
    gdim = cc.shape[0]
    ng = d // gdim
    oc, os_ = pl.pallas_call(
        _wfold_kernel,
        grid=(ng,),
        in_specs=[
            pl.BlockSpec((d, gdim), lambda g: (0, g)),
            pl.BlockSpec((gdim, gdim), lambda g: (0, 0)),
            pl.BlockSpec((gdim, gdim), lambda g: (0, 0)),
        ],
        out_specs=[pl.BlockSpec((d, gdim), lambda g: (0, g)), pl.BlockSpec((d, gdim), lambda g: (0, g))],
        out_shape=[jax.ShapeDtypeStruct((d, d), BF16), jax.ShapeDtypeStruct((d, d), BF16)],
        compiler_params=_cparams("parallel"),
        name="fourier_fold",
    )(w_in, cc, sc)
    return jnp.concatenate([oc, os_], axis=1)


def _fourier_mixer(h, nw, sc, sh, w_fold, tables, w_out_bf, g, nw_post, tag):
    b, t, d = h.shape
    half = t // 2
    tn = _tile(2 * d, 512)
    shapes, specs = _parity_out(b, t, 2 * d, tn, [BF16])
    a = _proj(h, nw, sc, sh, [(w_fold, 0)], n_col_tiles=2 * d // tn, tn=tn, tm=1024, epi=_epi_identity,
              out_shapes=shapes, out_specs=specs, parity=True, name="fourier_in_" + tag)[0]
    scale = 1.0 / math.sqrt(t * (d // FN_GROUPS))

    def epi(i, accs, aux_refs, out_refs):
        ev, od = accs
        out_refs[0][0, 0] = ((ev + od) * scale).astype(BF16)
        out_refs[0][0, 1] = ((ev - od) * scale).astype(BF16)

    a2 = a.reshape(2 * b, half, 2 * d)
    tn2 = _tile(d, 512)
    nj = d // tn2
    views = [(a2, lambda p, bi, par=par: par * b + bi, lambda p, j, q=q: q * nj + j)
             for par in (0, 1) for q in (0, 1)]
    f = _dft_call(
        tables, views, [(0, 0, 0), (0, 1, 1), (1, 2, 2), (1, 3, 3)], 2,
        n_par=1, batch=b, m=half, kdim=half, n=d, tm=512, tn=tn2, epi=epi,
        out_shapes=[jax.ShapeDtypeStruct((b, 2, half, d), BF16)],
        out_specs=lambda tm, tn_: [pl.BlockSpec((1, 2, tm, tn_), lambda p, i, j, bi: (bi, 0, i, j))],
        name="fourier_pos_" + tag)[0]
    return _outproj(f.reshape(b, t, d), w_out_bf, h, g, nw_post, name="fourier_out_" + tag)


def _rope_tables(n_tokens):
    rows = n_tokens // GRID_W
    row = jnp.broadcast_to(jnp.arange(rows, dtype=F32)[:, None], (rows, GRID_W)).reshape(n_tokens)
    col = jnp.broadcast_to(jnp.arange(GRID_W, dtype=F32)[None, :], (rows, GRID_W)).reshape(n_tokens)
    inv_freq = ROPE_THETA ** (-jnp.arange(ROPE_FREQS, dtype=F32) / ROPE_FREQS)
    ar, ac = row[:, None] * inv_freq, col[:, None] * inv_freq
    cos = jnp.concatenate([jnp.cos(ar), jnp.cos(ar), jnp.cos(ac), jnp.cos(ac)], axis=1)
    sin = jnp.concatenate([-jnp.sin(ar), jnp.sin(ar), -jnp.sin(ac), jnp.sin(ac)], axis=1)
    return cos, sin


def _attn_in_epi(j, ys, aux_refs, out_refs, rows, cols, *, n_q_tiles, rope):
    y = ys[0]
    o_ref = out_refs[0]
    gain = aux_refs[0][:, cols] * jnp.where(j < n_q_tiles, ATTN_SCALE, 1.0)
    t = y * lax.rsqrt(jnp.mean(y * y, axis=-1, keepdims=True) + RMS_EPS) * gain
    if rope:
        cos, sin = aux_refs[1][rows, :], aux_refs[2][rows, :]
        lane = lax.broadcasted_iota(jnp.int32, t.shape, 1)
        half = ROPE_FREQS
        partner = jnp.where(lane % (2 * half) < half,
                            pltpu.roll(t, HEAD_DIM - half, axis=1), pltpu.roll(t, half, axis=1))
        t = t * cos + partner * sin
    o_ref[0, rows, cols] = jnp.where(j <= n_q_tiles, t, y).astype(o_ref.dtype)


def _attn_project(h, nw, sc, sh, w_in_bf, gains, rope_tabs, tag):
    b, t, d = h.shape
    n = w_in_bf.shape[1]
    tn = N_KV_HEADS * HEAD_DIM
    n_q_tiles = d // tn
    rope = rope_tabs is not None
    aux = [gains]
    aux_specs = [pl.BlockSpec((1, tn), lambda bi, i, j: (0, jnp.minimum(j, n_q_tiles)))]
    tm = _tile(t, 1024, 16)
    if rope:
        aux += list(rope_tabs)
        aux_specs += [pl.BlockSpec((tm, HEAD_DIM), lambda bi, i, j: (i, 0))] * 2
    epi = functools.partial(_attn_in_epi, n_q_tiles=n_q_tiles, rope=rope)
    return _proj_piped(h, nw, sc, sh, w_in_bf, n_col_tiles=n // tn, tn=tn, tm=tm, epi=epi,
                       out_shape=jax.ShapeDtypeStruct((b, t, n), BF16),
                       out_spec=pl.BlockSpec((1, tm, tn), lambda bi, i, j: (bi, i, j)),
                       aux=aux, aux_specs=aux_specs, name="attn_in_" + tag)


def _attn_kernel(*refs, has_lat, tkv, n_lat_chunks):
    if has_lat:
        q_ref, kc_ref, vc_ref, k_ref, v_ref, o_ref = refs
    else:
        q_ref, kc_ref, vc_ref, o_ref = refs
    q = q_ref[0]
    n_chunks = 1 + (n_lat_chunks if has_lat else 0)

    def keys(c):
        return kc_ref[0] if c == 0 else k_ref[0, (c - 1) * tkv:c * tkv, :]

    def values(c):
        return vc_ref[0] if c == 0 else v_ref[0, (c - 1) * tkv:c * tkv, :]

    s = _dot_nt(q, keys(0))
    m = l = acc = None
    for c in range(n_chunks):
        s_max = jnp.max(s, axis=-1, keepdims=True)
        m_new = s_max if c == 0 else jnp.maximum(m, s_max)
        p = jnp.exp(s - m_new)
        p_sum = jnp.sum(p, axis=-1, keepdims=True)
        if c + 1 < n_chunks:
            s = _dot_nt(q, keys(c + 1))
        pv = _dot(p.astype(BF16), values(c))
        if c == 0:
            l, acc = p_sum, pv
        else:
            alpha = jnp.exp(m - m_new)
            l = alpha * l + p_sum
            acc = alpha * acc + pv
        m = m_new
    o_ref[0] = (acc / l).astype(o_ref.dtype)


def _attention(qkv, qkv_ctx, d, *, tq=512, tkv=1024):
    has_lat = qkv is not None
    src = qkv if has_lat else qkv_ctx
    b, t, _ = src.shape
    tc = qkv_ctx.shape[1]
    n_heads = d // HEAD_DIM
    grp = n_heads // N_KV_HEADS
    k_off = n_heads
    v_off = n_heads + N_KV_HEADS
    tq = _tile(t, tq, 16)
    tkv = _tile(t, tkv)
    ins = [src, qkv_ctx, qkv_ctx]
    specs = [
        pl.BlockSpec((1, tq, HEAD_DIM), lambda bi, hh, i: (bi, i, hh)),
        pl.BlockSpec((1, tc, HEAD_DIM), lambda bi, hh, i: (bi, 0, k_off + hh // grp)),
        pl.BlockSpec((1, tc, HEAD_DIM), lambda bi, hh, i: (bi, 0, v_off + hh // grp)),
    ]
    if has_lat:
        ins += [qkv, qkv]
        specs += [
            pl.BlockSpec((1, t, HEAD_DIM), lambda bi, hh, i: (bi, 0, k_off + hh // grp)),
            pl.BlockSpec((1, t, HEAD_DIM), lambda bi, hh, i: (bi, 0, v_off + hh // grp)),
        ]
    return pl.pallas_call(
        functools.partial(_attn_kernel, has_lat=has_lat, tkv=tkv, n_lat_chunks=t // tkv),
        grid=(b, n_heads, t // tq),
        in_specs=specs,
        out_specs=pl.BlockSpec((1, tq, HEAD_DIM), lambda bi, hh, i: (bi, i, hh)),
        out_shape=jax.ShapeDtypeStruct((b, t, d), BF16),
        compiler_params=_cparams("parallel", "parallel", "parallel"),
        name="attn_core_lat" if has_lat else "attn_core_ctx",
    )(*ins)


def _dt_epi(j, ys, aux_refs, out_refs, rows, cols):
    x = ys[0] + aux_refs[0][:, cols]
    dt = jnp.maximum(x, 0.0) + jnp.log(1.0 + jnp.exp(-jnp.abs(x)))
    out_refs[0][0, rows, cols] = dt
    out_refs[1][0, rows, cols] = dt * (-jnp.exp(aux_refs[1][:, cols]))


def _ssd_kernel(*refs, rev, with_output, final, nc):
    refs = list(refs)
    x_ref, b_ref, c_ref, dt_ref, adt_ref, s0_ref = [refs.pop(0) for _ in range(6)]
    if final:
        yprev_ref, z_ref, dskip_ref, nw_ref = [refs.pop(0) for _ in range(4)]
    y_ref = refs.pop(0) if with_output else None
    sfin_ref = refs.pop(0)
    st_ref = refs.pop(0)
    c = pl.program_id(2)
    q = SSM_CHUNK
    hp = SSM_HEAD_DIM
    ns = SSM_STATE
    gpb = dt_ref.shape[1]
    n_hg = dt_ref.shape[2]
    width = n_hg * hp

    @pl.when(c == 0)
    def _():
        st_ref[...] = s0_ref[0]

    ii = lax.broadcasted_iota(jnp.int32, (q, q), 0)
    jj = lax.broadcasted_iota(jnp.int32, (q, q), 1)
    mask = (jj >= ii) if rev else (jj <= ii)
    mask_bf = mask.astype(F32).astype(BF16)
    mask_t_bf = ((ii >= jj) if rev else (ii <= jj)).astype(F32).astype(BF16)
    eye_bf = (ii == jj).astype(F32).astype(BF16)
    lane = lax.broadcasted_iota(jnp.int32, (q, 2 * hp), 1)

    def hi_lo(v):
        hi = v.astype(BF16).astype(F32)
        return hi, v - hi

    def expand_rows(v):
        return jnp.concatenate([jnp.broadcast_to(v[hh:hh + 1, :], (hp, q)) for hh in range(n_hg)], axis=0)

    groups = range(gpb)
    cols = [slice(gi * width, (gi + 1) * width) for gi in groups]
    scols = [slice(gi * ns, (gi + 1) * ns) for gi in groups]
    dt_r = [dt_ref[0, gi] for gi in groups]
    a_parts = [hi_lo(adt_ref[0, gi]) for gi in groups]
    d_parts = [hi_lo(dt_r[gi]) for gi in groups]
    e_acs = [_dot_nt(mask_bf, expand_rows(a_parts[gi][0]).astype(BF16))
             + _dot_nt(mask_bf, expand_rows(a_parts[gi][1]).astype(BF16)) for gi in groups]
    e_dt = [_dot_nt(eye_bf, expand_rows(d_parts[gi][0]).astype(BF16))
            + _dot_nt(eye_bf, expand_rows(d_parts[gi][1]).astype(BF16)) for gi in groups]
    last = [e_acs[gi][0:1, :] if rev else e_acs[gi][q - 1:q, :] for gi in groups]

    x = [x_ref[0, :, cols[gi]] for gi in groups]
    bm = [b_ref[0, :, scols[gi]] for gi in groups]
    s_old = [st_ref[gi] for gi in groups]
    s_new = []
    for gi in groups:
        wend = jnp.exp(last[gi] - e_acs[gi]) * e_dt[gi]
        s_new.append(s_old[gi] * jnp.exp(last[gi]) + _dot(bm[gi].T.astype(BF16), (x[gi] * wend).astype(BF16)))

    def store_states():
        for gi in groups:
            st_ref[gi] = s_new[gi]

        @pl.when(c == nc - 1)
        def _():
            for gi in groups:
                sfin_ref[0, gi] = s_new[gi]

    if not with_output:
        store_states()
        return
    cm_bf = [c_ref[0, :, scols[gi]].astype(BF16) for gi in groups]
    cb = [_dot_nt(cm_bf[gi], bm[gi].astype(BF16)) for gi in groups]
    acs_r = []
    for gi in groups:
        r16 = _dot(jnp.concatenate(a_parts[gi], axis=0).astype(BF16), mask_t_bf)
        acs_r.append(r16[0:n_hg] + r16[n_hg:2 * n_hg])
    y_state = [_dot(cm_bf[gi], s_old[gi].astype(BF16)) * jnp.exp(e_acs[gi]) for gi in groups]
    ys = [[] for _ in groups]
    for pair in range(n_hg // 2):
        for gi in groups:
            ws = []
            for hh in (2 * pair, 2 * pair + 1):
                seg = e_acs[gi][:, hh * hp:hh * hp + 1] - acs_r[gi][hh:hh + 1, :]
                decay = jnp.exp(jnp.where(mask, seg, NEG_BIG))
                ws.append((cb[gi] * decay * dt_r[gi][hh:hh + 1, :]).astype(BF16))
            xp = x[gi][:, pair * 2 * hp:(pair + 1) * 2 * hp]
            bd = jnp.concatenate([jnp.where(lane < hp, xp, 0.0), jnp.where(lane >= hp, xp, 0.0)], axis=0)
            ys[gi].append(_dot(jnp.concatenate(ws, axis=1), bd.astype(BF16)))
    for gi in groups:
        y = jnp.concatenate(ys[gi], axis=1) + y_state[gi]
        if final:
            yt = yprev_ref[0, :, cols[gi]] + y + x[gi] * dskip_ref[:, cols[gi]]
            yt = yt * _silu(z_ref[0, :, cols[gi]])
            y_ref[0, :, cols[gi]] = _rms_rows(yt, nw_ref[:, cols[gi]]).astype(y_ref.dtype)
        else:
            y_ref[0, :, cols[gi]] = y
    store_states()


def _ssd_scan(xbc, dt_rows, adt_rows, s0, d_inner, *, rev, with_output, final_args=None):
    b, t, _ = xbc.shape
    g = SSM_GROUPS
    q = SSM_CHUNK
    nc = t // q
    width = d_inner // g
    n_hg = width // SSM_HEAD_DIM
    gpb = SSD_GROUPS_PER_STEP
    ngb = g // gpb
    wblk = gpb * width
    sblk = gpb * SSM_STATE
    b_off = d_inner // sblk
    c_off = b_off + ngb
    final = final_args is not None
    cmap = (lambda c: nc - 1 - c) if rev else (lambda c: c)
    ins = [xbc, xbc, xbc, dt_rows, adt_rows, s0]
    specs = [
        pl.BlockSpec((1, q, wblk), lambda bi, gi, c: (bi, cmap(c), gi)),
        pl.BlockSpec((1, q, sblk), lambda bi, gi, c: (bi, cmap(c), b_off + gi)),
        pl.BlockSpec((1, q, sblk), lambda bi, gi, c: (bi, cmap(c), c_off + gi)),
        pl.BlockSpec((1, gpb, n_hg, q), lambda bi, gi, c: (bi, gi, 0, cmap(c))),
        pl.BlockSpec((1, gpb, n_hg, q), lambda bi, gi, c: (bi, gi, 0, cmap(c))),
        pl.BlockSpec((1, gpb, SSM_STATE, width), lambda bi, gi, c: (bi, gi, 0, 0)),
    ]
    if final:
        yprev, z, dskip, nw = final_args
        ins += [yprev, z, dskip, nw]
        specs += [
            pl.BlockSpec((1, q, wblk), lambda bi, gi, c: (bi, cmap(c), gi)),
            pl.BlockSpec((1, q, wblk), lambda bi, gi, c: (bi, cmap(c), gi)),
            pl.BlockSpec((1, wblk), lambda bi, gi, c: (0, gi)),
            pl.BlockSpec((1, wblk), lambda bi, gi, c: (0, gi)),
        ]
    out_shapes, out_specs = [], []
    if with_output:
        out_shapes.append(jax.ShapeDtypeStruct((b, t, d_inner), BF16 if final else F32))
        out_specs.append(pl.BlockSpec((1, q, wblk), lambda bi, gi, c: (bi, cmap(c), gi)))
    out_shapes.append(jax.ShapeDtypeStruct(s0.shape, F32))
    out_specs.append(pl.BlockSpec((1, gpb, SSM_STATE, width), lambda bi, gi, c: (bi, gi, 0, 0)))
    return pl.pallas_call(
        functools.partial(_ssd_kernel, rev=rev, with_output=with_output, final=final, nc=nc),
        grid=(b, ngb, nc),
        in_specs=specs,
        out_specs=out_specs,
        out_shape=out_shapes,
        scratch_shapes=[pltpu.VMEM((gpb, SSM_STATE, width), F32)],
        compiler_params=_cparams("parallel", "parallel", "arbitrary"),
        name="ssd_scan_%s%s" % ("bwd" if rev else "fwd", "" if with_output else "_state"),
    )(*ins)


def _ssd_project(h, nw, sc, sh, wz, wx, wdt, cw, cb, dt_bias, a_log, need_z, tag):
    b, t, d = h.shape
    d_inner = wz.shape[1]
    z = None
    if need_z:
        tn = _tile(d_inner, 512)
        shapes, specs = _std_out(b, t, d_inner, tn, F32)
        z = _proj(h, nw, sc, sh, [(wz, 0)], n_col_tiles=d_inner // tn, tn=tn, tm=1024, epi=_epi_identity,
                  out_shapes=shapes, out_specs=specs, name="ssd_in_z_" + tag)[0]
    nx = wx.shape[1]
    tn = _tile(nx, 512)
    shapes, specs = _std_out(b, t, nx, tn, F32)
    xbc = _proj(h, nw, sc, sh, [(wx, 0)], n_col_tiles=nx // tn, tn=tn, tm=1024, epi=_epi_silu,
                conv=(cw, cb), out_shapes=shapes, out_specs=specs, name="ssd_in_xbc_" + tag)[0]
    nd = wdt.shape[1]
    shapes = [jax.ShapeDtypeStruct((b, t, nd), F32)] * 2
    specs = lambda tm: [pl.BlockSpec((1, tm, nd), lambda bi, i, j: (bi, i, 0))] * 2
    dt, adt = _proj(h, nw, sc, sh, [(wdt, 0)], n_col_tiles=1, tn=nd, tm=1024, epi=_dt_epi,
                    out_shapes=shapes, out_specs=specs,
                    aux=[dt_bias.reshape(1, nd), a_log.reshape(1, nd)],
                    aux_specs=[pl.BlockSpec((1, nd), lambda bi, i, j: (0, 0))] * 2, name="ssd_in_dt_" + tag)

    def rows(v):
        return jnp.transpose(v.reshape(b, t, 2, SSM_GROUPS, -1), (2, 0, 3, 4, 1))

    return z, xbc, rows(dt), rows(adt)


def _ssd_mixer(h, hc, mods, mods_c, nw_pre, p, g, nw_post):
    b, t, d = h.shape
    wz, wx, wdt = p["wz"], p["wx"], p["wdt"]
    d_inner = wz.shape[1]
    sc, sh = mods
    scc, shc = mods_c
    z, xbc, dt_r, adt_r = _ssd_project(h, nw_pre, sc, sh, wz, wx, wdt, p["cw"], p["cb"], p["dt_bias"],
                                       p["a_log"], True, "lat")
    _, xbc_c, dt_rc, adt_rc = _ssd_project(hc, nw_pre, scc, shc, wz, wx, wdt, p["cw"], p["cb"], p["dt_bias"],
                                           p["a_log"], False, "ctx")
    s0 = jnp.zeros((b, SSM_GROUPS, SSM_STATE, d_inner // SSM_GROUPS), F32)
    (s_f,) = _ssd_scan(xbc_c, dt_rc[0], adt_rc[0], s0, d_inner, rev=False, with_output=False)
    (s_b,) = _ssd_scan(xbc_c, dt_rc[1], adt_rc[1], s0, d_inner, rev=True, with_output=False)
    y_f, _ = _ssd_scan(xbc, dt_r[0], adt_r[0], s_f, d_inner, rev=False, with_output=True)
    dskip = jnp.repeat(p["d_skip"], SSM_HEAD_DIM).reshape(1, d_inner)
    yn, _ = _ssd_scan(xbc, dt_r[1], adt_r[1], s_b, d_inner, rev=True, with_output=True,
                      final_args=(y_f, z, dskip, p["norm_w"].reshape(1, d_inner)))
    return _outproj(yn, p["w_out"], h, g, nw_post, name="ssd_out")


def _hyena_tables(n):
    half = n // 2
    k, s = _int_grid(half)
    def trig(m):
        ang = (m % (2 * n)).astype(F32) * (math.pi / n)
        return jnp.cos(ang), jnp.sin(ang)
    ce, se = trig(2 * s * k)
    co, so = trig((2 * s + 1) * k)
    cot, sot = trig((2 * k + 1) * s)
    alt_s = jnp.where(s % 2 == 0, 1.0, -1.0)
    alt_k = jnp.where(k % 2 == 0, 1.0, -1.0)
    sep = jnp.where(k == 0, alt_s, se)
    sop = jnp.where(k == 0, alt_s, so)
    g = jnp.where(k == 0, alt_s, -se)
    nsept = jnp.where(s == 0, -alt_k, -se)
    nsopt = jnp.where(s == 0, -alt_k, -sot)
    bf = lambda *ts: jnp.stack([t.astype(BF16) for t in ts])
    return dict(ce=bf(ce), co=bf(co), sep=bf(sep), sop=bf(sop), g=bf(g), nsop=bf(-sop),
                inv_c=bf(ce, cot), inv_s=bf(nsept, nsopt))


def _hyena_feats(n):
    t01 = jnp.linspace(0.0, 1.0, n, dtype=F32)[:, None]
    w = (2.0 * math.pi / n) * jnp.arange(n, dtype=F32)[:, None]
    f = jnp.linspace(1e-4, HY_BANDS - 1, HY_BANDS, dtype=F32)[None, :]
    feats = jnp.concatenate([t01, jnp.cos(f * w), -jnp.sin(f * w)], axis=-1)
    return jnp.pad(feats, ((0, 0), (0, LANES - HY_EMB)))


def _taps_kernel(feats_ref, w1_ref, b1_ref, w2_ref, b2_ref, fr_ref, w3_ref, dec_ref, o_ref, hid_ref, *,
                 tiles_per_dir):
    j = pl.program_id(0)

    @pl.when(j == 0)
    def _():
        fr = fr_ref[...]
        h1 = jnp.sin(fr * (_dot3(feats_ref[...], w1_ref[...]) + b1_ref[...]))
        hid_ref[...] = jnp.sin(fr * (_dot3(h1, w2_ref[...]) + b2_ref[...]))

    taps = _dot3(hid_ref[...], w3_ref[...])
    taps = taps * jnp.exp(-feats_ref[:, 0:1] * jnp.abs(dec_ref[...]))
    is_bwd = (j // tiles_per_dir) % 2 == 1
    row = lax.broadcasted_iota(jnp.int32, taps.shape, 0)
    o_ref[0] = jnp.where((row == 0) & is_bwd, 0.0, taps).astype(o_ref.dtype)


def _hyena_taps(n, p, d):
    feats = _hyena_feats(n)
    tn = _tile(d, 256)
    tpd = d // tn
    ncols = HY_ORDER * 2 * d

    def omap(j):
        return ((j // tpd) % 2, 0, (j // (2 * tpd)) * tpd + j % tpd)

    full = lambda j: (0, 0)
    return pl.pallas_call(
        functools.partial(_taps_kernel, tiles_per_dir=tpd),
        grid=(ncols // tn,),
        in_specs=[
            pl.BlockSpec((n, LANES), full),
            pl.BlockSpec((LANES, LANES), full), pl.BlockSpec((1, LANES), full),
            pl.BlockSpec((LANES, LANES), full), pl.BlockSpec((1, LANES), full),
            pl.BlockSpec((1, LANES), full),
            pl.BlockSpec((LANES, tn), lambda j: (0, j)),
            pl.BlockSpec((1, tn), lambda j: (0, j)),
        ],
        out_specs=pl.BlockSpec((1, n, tn), omap),
        out_shape=jax.ShapeDtypeStruct((2, n, HY_ORDER * d), BF16),
        scratch_shapes=[pltpu.VMEM((n, LANES), F32)],
        compiler_params=_cparams("arbitrary"),
        name="hyena_taps",
    )(feats, p["fw1"], p["fb1"], p["fw2"], p["fb2"], p["ffreq"], p["fw3"], p["decay"])


def _first_row(shape, i):
    return (lax.broadcasted_iota(jnp.int32, shape, 0) == 0) & (i == 0)


def _hyena_spectrum(taps, tabs, n, d):
    half = n // 2
    ncol = HY_ORDER * d
    taps2 = taps.reshape(2, half, 2 * ncol)
    tn = _tile(ncol, 512)
    nj = ncol // tn
    views = [(taps2, lambda p, b, dr=dr: dr, lambda p, j, par=par: par * nj + j)
             for dr in (0, 1) for par in (0, 1)]
    l_tabs = [tabs["ce"], tabs["co"], tabs["sep"], tabs["sop"], tabs["g"], tabs["nsop"]]
    terms = [(0, 0, 0), (0, 0, 2), (1, 1, 1), (1, 1, 3), (2, 2, 0), (2, 4, 2), (3, 3, 1), (3, 5, 3)]

    def epi(i, accs, aux_refs, out_refs):
        pec, poc, pes, pos = accs
        first = _first_row(pec.shape, i)
        out_refs[0][...] = pec + poc
        out_refs[1][...] = jnp.where(first, pes, -(pes + pos))
        out_refs[2][...] = pec - poc
        out_refs[3][...] = jnp.where(first, -pos, pes - pos)

    return _dft_call(
        l_tabs, views, terms, 4, n_par=1, batch=1, m=half, kdim=half, n=ncol, tm=256, tn=tn, epi=epi,
        out_shapes=[jax.ShapeDtypeStruct((half, ncol), F32)] * 4,
        out_specs=lambda tm, tn_: [pl.BlockSpec((tm, tn_), lambda p, i, j, b: (i, j))] * 4,
        name="hyena_spectrum")


def _hyena_long_conv(src_bf, src_f32, gate, gate_off, spec, order, bias, tabs, d, out_dtypes, tag):
    b2, half, _ = src_bf.shape
    b = b2 // 2
    n = 2 * half
    inv_n = 1.0 / n
    tn = _tile(d, 512)
    nj = d // tn

    def fwd_epi(i, accs, aux_refs, out_refs):
        pec, poc, pes, pos = accs
        hr, hi, hur, hui = [a[...] for a in aux_refs]
        vr, ur = pec + poc, pec - poc
        vi, ui = -(pes + pos), pes - pos
        yr, yi = vr * hr - vi * hi, vr * hi + vi * hr
        yur, yui = ur * hur - ui * hui, ur * hui + ui * hur
        first = _first_row(pec.shape, i)
        y0, yny = vr * hr, ur * hur
        mr, mi = pes, -pos
        ymr, ymi = mr * hi - mi * hui, mr * hui + mi * hi
        ae = jnp.where(first, 0.5 * (y0 + yny), yr + yur)
        be = jnp.where(first, -ymr, yi - yui)
        ao = jnp.where(first, 0.5 * (y0 - yny), yr - yur)
        bo = jnp.where(first, ymi, yi + yui)
        for q, v in enumerate((ae, be, ao, bo)):
            out_refs[0][0, :, q * tn:(q + 1) * tn] = (v * inv_n).astype(BF16)

    views = [(src_bf, lambda p, bi: bi, lambda p, j: j), (src_bf, lambda p, bi: b + bi, lambda p, j: j)]
    yab = _dft_call(
        [tabs["ce"], tabs["co"], tabs["sep"], tabs["sop"]], views,
        [(0, 0, 0), (1, 1, 1), (2, 2, 0), (3, 3, 1)], 4,
        n_par=1, batch=b, m=half, kdim=half, n=d, tm=512, tn=tn, epi=fwd_epi,
        out_shapes=[jax.ShapeDtypeStruct((b, half, 4 * d), BF16)],
        out_specs=lambda tm, tn_: [pl.BlockSpec((1, tm, 4 * tn_), lambda p, i, j, bi: (bi, i, j))],
        aux=list(spec),
        aux_specs=lambda tm, tn_: [pl.BlockSpec((tm, tn_), lambda p, i, j, bi: (i, order * nj + j))] * 4,
        name="hyena_fwd_" + tag)[0]

    def inv_epi(i, accs, aux_refs, out_refs):
        val = aux_refs[1][0] * (accs[0] + aux_refs[0][0] * aux_refs[2][...])
        for o_ref in out_refs:
            o_ref[0] = val.astype(o_ref.dtype)

    return _dft_call(
        [tabs["inv_c"], tabs["inv_s"]],
        [(yab, lambda p, bi: bi, lambda p, j: 4 * j + 2 * p), (yab, lambda p, bi: bi, lambda p, j: 4 * j + 2 * p + 1)],
        [(0, 0, 0), (0, 1, 1)], 1,
        n_par=2, batch=b, m=half, kdim=half, n=d, tm=1024, tn=tn, epi=inv_epi,
        out_shapes=[jax.ShapeDtypeStruct((2 * b, half, d), dt) for dt in out_dtypes],
        out_specs=lambda tm, tn_: [pl.BlockSpec((1, tm, tn_), lambda p, i, j, bi: (p * b + bi, i, j))
                                   for _ in out_dtypes],
        aux=[src_f32, gate, bias],
        aux_specs=lambda tm, tn_: [
            pl.BlockSpec((1, tm, tn_), lambda p, i, j, bi: (p * b + bi, i, j)),
            pl.BlockSpec((1, tm, tn_), lambda p, i, j, bi: (p * b + bi, i, gate_off * nj + j)),
            pl.BlockSpec((1, tn_), lambda p, i, j, bi: (0, j)),
        ],
        name="hyena_inv_" + tag)


def _hyena_mixer(h, mods, nw_pre, p, g, nw_post):
    b, t, d = h.shape
    sc, sh = mods
    tn = _tile(3 * d, 512)
    half = t // 2
    shapes, specs = _parity_out(b, t, 3 * d, tn, [F32, BF16])
    pj, pj_bf = _proj(h, nw_pre, sc, sh, [(p["w_in"], 0)], n_col_tiles=3 * d // tn, tn=tn, tm=1024,
                      epi=_epi_identity, conv=(p["cw"], p["cb"]), out_shapes=shapes, out_specs=specs,
                      parity=True, name="hyena_in")
    tabs = _hyena_tables(t)
    taps = _hyena_taps(t, p, d)
    spec = _hyena_spectrum(taps, tabs, t, d)
    fbias = p["fbias"]
    pj2 = pj.reshape(2 * b, half, 3 * d)
    z, z_bf = _hyena_long_conv(pj_bf.reshape(2 * b, half, 3 * d), pj2, pj2, 1, spec, 0, fbias[0:1], tabs, d,
                               (F32, BF16), "o1")
    (y,) = _hyena_long_conv(z_bf, z, pj2, 2, spec, 1, fbias[1:2], tabs, d, (BF16,), "o2")
    return _outproj(y.reshape(2, b, half, d), p["w_out"], h, g, nw_post, parity=True, name="hyena_out")


def _conv_ffn(h, sc, sh, g, nw_pre, nw_post, w_up_bf, cw, cb, w_down_bf, tag):
    b, t, d = h.shape
    dff = w_down_bf.shape[0]
    tn = _tile(dff, 512)
    nj = dff // tn
    shapes, specs = _std_out(b, t, dff, tn, BF16)
    act = _proj(h, nw_pre, sc, sh, [(w_up_bf, 0), (w_up_bf, nj)], n_col_tiles=nj, tn=tn, tm=1024,
                epi=_epi_swiglu, conv=(cw, cb.reshape(1, -1)), out_shapes=shapes, out_specs=specs,
                name="ffn_up_" + tag)[0]
    return _outproj(act, w_down_bf, h, g, nw_post, name="ffn_down_" + tag)


def kernel(x, c, ctx, c_ctx, ada_w, ada_b, norm_pre_mix, norm_post_mix, norm_pre_ffn, norm_post_ffn, ffn_up, ffn_conv_w, ffn_conv_b, ffn_down, fn_in, fn_out, attn_in, attn_q_gain, attn_k_gain, attn_out, ssm_in, ssm_conv_w, ssm_conv_b, ssm_dt_bias, ssm_a_log, ssm_d, ssm_norm, ssm_out, hy_in, hy_conv_w, hy_conv_b, hy_f_w1, hy_f_b1, hy_f_w2, hy_f_b2, hy_f_w3, hy_f_freq, hy_decay, hy_f_bias, hy_out):
    b, t, d = x.shape
    tc = ctx.shape[1]
    depth = ada_w.shape[0]
    readers = [i for i in range(depth) if i % N_MIXERS in CTX_READER_KINDS]
    last_reader = max(readers) if readers else -1

    n_rows = -(-(b + 1) // 8) * 8
    c_rows = jnp.zeros((n_rows, d), F32).at[:b].set(c).at[b].set(c_ctx)
    mod = _modulation_all(c_rows, ada_w, ada_b).reshape(depth, n_rows, 6, d)

    h, hc = x, ctx
    for i in range(depth):
        kind, j = i % N_MIXERS, i // N_MIXERS
        ctx_live = i <= last_reader
        ctx_next = i < last_reader
        m_lat = [mod[i, :b, s][:, None, :] for s in range(6)]
        sh1, sc1, g1, sh2, sc2, g2 = m_lat
        if ctx_live:
            m_ctx = [jnp.broadcast_to(mod[i, b, s][None, None, :], (b, 1, d)) for s in range(6)]
            csh1, csc1, cg1, csh2, csc2, cg2 = m_ctx
        nw_pre, nw_post = norm_pre_mix[i], norm_post_mix[i]
        if kind == 0:
            gdim = d // FN_GROUPS
            cc, scs = _fourier_chan_tables(gdim)
            w_fold = _fold_channel_dft(fn_in[j], cc, scs)
            w_out_bf = fn_out[j].astype(BF16)
            h = _fourier_mixer(h, nw_pre, sc1, sh1, w_fold, _fourier_pos_tables(t), w_out_bf, g1, nw_post, "lat")
            if ctx_next:
                hc = _fourier_mixer(hc, nw_pre, csc1, csh1, w_fold, _fourier_pos_tables(tc), w_out_bf, cg1,
                                    nw_post, "ctx")
        elif kind == 1:
            w_in_bf = attn_in[j].astype(BF16)
            w_out_bf = attn_out[j].astype(BF16)
            n_q, n_kv = d // HEAD_DIM, N_KV_HEADS
            gains = jnp.concatenate([jnp.tile(attn_q_gain[j], n_q), jnp.tile(attn_k_gain[j], n_kv)]).reshape(1, -1)
            qkv = _attn_project(h, nw_pre, sc1, sh1, w_in_bf, gains, _rope_tables(t), "lat")
            qkv_c = _attn_project(hc, nw_pre, csc1, csh1, w_in_bf, gains, None, "ctx")
            o = _attention(qkv, qkv_c, d)
            h = _outproj(o, w_out_bf, h, g1, nw_post, name="attn_out_lat")
            if ctx_next:
                oc = _attention(None, qkv_c, d)
                hc = _outproj(oc, w_out_bf, hc, cg1, nw_post, name="attn_out_ctx")
        elif kind == 2:
            w = ssm_in[j]
            d_inner = ssm_out.shape[1]
            n_conv = ssm_conv_w.shape[2]
            p = dict(wz=w[:, :d_inner].astype(BF16), wx=w[:, d_inner:d_inner + n_conv].astype(BF16),
                     wdt=w[:, d_inner + n_conv:].astype(BF16), cw=ssm_conv_w[j], cb=ssm_conv_b[j].reshape(1, -1),
                     dt_bias=ssm_dt_bias[j], a_log=ssm_a_log[j], d_skip=ssm_d[j], norm_w=ssm_norm[j],
                     w_out=ssm_out[j].astype(BF16))
            if ctx_next:
                raise NotImplementedError("context output of the SSD mixer is not needed at this depth")
            h = _ssd_mixer(h, hc, (sc1, sh1), (csc1, csh1), nw_pre, p, g1, nw_post)
        else:
            pad_r = LANES - hy_f_w1.shape[1]
            pad_c = LANES - hy_f_w1.shape[2]
            p = dict(w_in=hy_in[j].astype(BF16), cw=hy_conv_w[j], cb=hy_conv_b[j].reshape(1, -1),
                     fw1=jnp.pad(hy_f_w1[j], ((0, pad_r), (0, pad_c))),
                     fb1=jnp.pad(hy_f_b1[j], (0, pad_c)).reshape(1, LANES),
                     fw2=jnp.pad(hy_f_w2[j], ((0, pad_c), (0, pad_c))),
                     fb2=jnp.pad(hy_f_b2[j], (0, pad_c)).reshape(1, LANES),
                     ffreq=jnp.pad(hy_f_freq[j], (0, pad_c)).reshape(1, LANES),
                     fw3=jnp.pad(hy_f_w3[j], ((0, pad_c), (0, 0))), decay=hy_decay[j].reshape(1, -1),
                     fbias=hy_f_bias[j], w_out=hy_out[j].astype(BF16))
            if ctx_next:
                raise NotImplementedError("context output of the Hyena mixer is not needed at this depth")
            h = _hyena_mixer(h, (sc1, sh1), nw_pre, p, g1, nw_post)
        w_up_bf = ffn_up[i].astype(BF16)
        w_down_bf = ffn_down[i].astype(BF16)
        h = _conv_ffn(h, sc2, sh2, g2, norm_pre_ffn[i], norm_post_ffn[i], w_up_bf, ffn_conv_w[i], ffn_conv_b[i],
                      w_down_bf, "lat")
        if ctx_next:
            hc = _conv_ffn(hc, csc2, csh2, cg2, norm_pre_ffn[i], norm_post_ffn[i], w_up_bf, ffn_conv_w[i],
                           ffn_conv_b[i], w_down_bf, "ctx")
    return h
```
